```python
import math
import functools
import jax
import jax.numpy as jnp
from jax import lax
import numpy as np

D_MODEL = 2048
BATCH = 8
SEQ = 2048
DEPTH = 2
DEC_BATCH = 32
DEC_SEQ = 4
PAST_LEN = 8192
PAGE_SIZE = 128

HEAD_DIM = 128
GDN_HEADS = D_MODEL // (2 * HEAD_DIM)
MOBA_HEADS = D_MODEL // (2 * HEAD_DIM)
GDN_WIDTH = GDN_HEADS * HEAD_DIM
MOBA_WIDTH = MOBA_HEADS * HEAD_DIM
MIX_WIDTH = GDN_WIDTH + MOBA_WIDTH
N_IN = 4 * GDN_WIDTH + 2 * GDN_HEADS + 3 * MOBA_WIDTH
GDN_CONV = 4
GDN_CHUNK = 64
MOBA_BLOCK = 256
MOBA_TOPK = 3
MOBA_QBLOCK = 16
ROPE_THETA = 500000.0
ROPE_DIM = HEAD_DIM // 4
D_FF = 11 * D_MODEL // 4
FFN_CONV = 3
ALPHA = (2 * DEPTH) ** 0.25
BETA_INIT = (8 * DEPTH) ** -0.25
LN_EPS = 1e-5
NORM_EPS = 1e-6
F32 = jnp.float32

kernel_name = "hymba_gdn_moba_convffn_deepnorm_adaln_step"


def _layer_norm(x, g, b):
    xf = x.astype(F32)
    mu = jnp.mean(xf, axis=-1, keepdims=True)
    var = jnp.mean(jnp.square(xf - mu), axis=-1, keepdims=True)
    return ((xf - mu) * lax.rsqrt(var + LN_EPS) * g + b).astype(x.dtype)


def _rms_norm(x, g):
    xf = x.astype(F32)
    return (xf * lax.rsqrt(jnp.mean(xf * xf, axis=-1, keepdims=True) + NORM_EPS) * g).astype(x.dtype)


def _l2norm(x):
    xf = x.astype(F32)
    return xf * lax.rsqrt(jnp.sum(xf * xf, axis=-1, keepdims=True) + NORM_EPS)


def _adaln(c, w, b):
    mod = (c @ w + b)[:, None, :]
    return jnp.split(mod, 3, axis=-1)


def _causal_dwconv(x, buf, w):
    width, t = w.shape[0], x.shape[1]
    xp = jnp.concatenate([buf.astype(x.dtype), x], axis=1)
    y = sum(xp[:, i:i + t] * w[i] for i in range(width))
    return y, xp[:, xp.shape[1] - (width - 1):]


def _partial_rope(x, pos):
    half = ROPE_DIM // 2
    inv_freq = ROPE_THETA ** (-jnp.arange(half, dtype=F32) / half)
    ang = pos.astype(F32)[:, None] * inv_freq[None, :]
    cos = jnp.cos(ang)[None, :, None, :]
    sin = jnp.sin(ang)[None, :, None, :]
    xr = x[..., :ROPE_DIM].astype(F32)
    x1, x2 = xr[..., :half], xr[..., half:]
    rot = jnp.concatenate([x1 * cos - x2 * sin, x2 * cos + x1 * sin], axis=-1).astype(x.dtype)
    return jnp.concatenate([rot, x[..., ROPE_DIM:]], axis=-1)


def _split_in(proj):
    sizes = (GDN_WIDTH,) * 4 + (GDN_HEADS,) * 2 + (MOBA_WIDTH,) * 3
    points = [sum(sizes[:i + 1]) for i in range(len(sizes) - 1)]
    return jnp.split(proj, points, axis=-1)


def _gated_delta_chunked(q, k, v, g, beta, s0, chunk):
    b, t, h, dk = q.shape
    dv = v.shape[-1]
    n = t // chunk

    def to_chunks(a):
        a = a.astype(F32).reshape((b, n, chunk, h) + a.shape[3:])
        return jnp.moveaxis(a, (1, 3), (0, 2))

    qc, kc, vc, gc, bc = (to_chunks(a) for a in (q, k, v, g, beta))
    gcum = jnp.cumsum(gc, axis=-1)
    tri_strict = jnp.tril(jnp.ones((chunk, chunk), bool), -1)
    tri_incl = jnp.tril(jnp.ones((chunk, chunk), bool))
    decay = jnp.exp(jnp.where(tri_incl, gcum[..., :, None] - gcum[..., None, :], -jnp.inf))
    kb = kc * bc[..., None]
    a_low = jnp.where(tri_strict, jnp.einsum('nbhid,nbhjd->nbhij', kb, kc) * decay, 0.0)
    rhs = jnp.concatenate([vc * bc[..., None], kb * jnp.exp(gcum)[..., None]], axis=-1)
    sol = lax.linalg.triangular_solve(jnp.eye(chunk, dtype=F32) + a_low, rhs,
                                      left_side=True, lower=True, unit_diagonal=True)
    u, w = sol[..., :dv], sol[..., dv:]
    qk = jnp.where(tri_incl, jnp.einsum('nbhid,nbhjd->nbhij', qc, kc) * decay, 0.0)
    qg = qc * jnp.exp(gcum)[..., None]
    kdec = kc * jnp.exp(gcum[..., -1:] - gcum)[..., None]
    glast = jnp.exp(gcum[..., -1])

    def step(s, xs):
        u_i, w_i, qk_i, qg_i, kd_i, gl_i = xs
        v_new = u_i - jnp.einsum('bhcd,bhde->bhce', w_i, s)
        o = jnp.einsum('bhcd,bhde->bhce', qg_i, s) + jnp.einsum('bhij,bhje->bhie', qk_i, v_new)
        s = s * gl_i[..., None, None] + jnp.einsum('bhcd,bhce->bhde', kd_i, v_new)
        return s, o

    s, o = lax.scan(step, s0.astype(F32), (u, w, qk, qg, kdec, glast))
    return jnp.moveaxis(o, (0, 2), (1, 3)).reshape(b, t, h, dv), s


def _gdn_mixer(q, k, v, z, b_logit, a_logit, conv_buf, s0, conv_w, a_log, dt_bias, norm_g, chunk):
    bsz, t, _ = q.shape
    qkv, new_buf = _causal_dwconv(jnp.concatenate([q, k, v], axis=-1), conv_buf, conv_w)
    qkv = jax.nn.silu(qkv)
    q, k, v = (a.reshape(bsz, t, GDN_HEADS, HEAD_DIM) for a in jnp.split(qkv, 3, axis=-1))
    q = _l2norm(q) * (HEAD_DIM ** -0.5)
    k = _l2norm(k)
    beta = jax.nn.sigmoid(b_logit.astype(F32))
    g = -jnp.exp(a_log.astype(F32)) * jax.nn.softplus(a_logit.astype(F32) + dt_bias.astype(F32))
    o, s = _gated_delta_chunked(q, k, v, g, beta, s0, chunk)
    o = _rms_norm(o, norm_g) * jax.nn.silu(z.reshape(bsz, t, GDN_HEADS, HEAD_DIM).astype(F32))
    return o.reshape(bsz, t, GDN_WIDTH).astype(z.dtype), new_buf, s


def _moba_prompt(q, k, v):
    b, t, h, hd = q.shape
    nb = -(-t // MOBA_BLOCK)
    pad = nb * MOBA_BLOCK - t

    def blocks(a):
        a = jnp.pad(a, ((0, 0), (0, pad), (0, 0), (0, 0)))
        return a.reshape(b, nb, MOBA_BLOCK, h, hd).transpose(0, 3, 1, 2, 4)

    kb, vb = blocks(k), blocks(v)
    kmean = jnp.mean(kb.astype(F32), axis=3)
    ksel = min(MOBA_TOPK, nb)
    nq = t // MOBA_QBLOCK
    scale = hd ** -0.5
    bi = jnp.arange(b)[:, None, None, None]
    hi = jnp.arange(h)[None, :, None, None]
    qs = q.reshape(b, nq, MOBA_QBLOCK, h, hd).transpose(1, 0, 3, 2, 4)

    def one(args):
        qi, i = args
        qpos = i * MOBA_QBLOCK + jnp.arange(MOBA_QBLOCK)
        cur = (i * MOBA_QBLOCK) // MOBA_BLOCK
        gate = jnp.einsum('bhqd,bhnd->bhqn', qi, kmean, preferred_element_type=F32)
        gate = jnp.where(jnp.arange(nb) < cur, gate, -jnp.inf)
        _, idx = lax.top_k(gate, ksel)
        valid = jnp.arange(ksel) < cur
        kg = kb[bi, hi, idx]
        vg = vb[bi, hi, idx]
        s_sel = jnp.einsum('bhqd,bhqnkd->bhqnk', qi, kg, preferred_element_type=F32) * scale
        s_sel = jnp.where(valid[:, None], s_sel, -jnp.inf)
        k_own = lax.dynamic_index_in_dim(kb, cur, axis=2, keepdims=False)
        v_own = lax.dynamic_index_in_dim(vb, cur, axis=2, keepdims=False)
        kpos = cur * MOBA_BLOCK + jnp.arange(MOBA_BLOCK)
        s_own = jnp.einsum('bhqd,bhkd->bhqk', qi, k_own, preferred_element_type=F32) * scale
        s_own = jnp.where(kpos[None, :] <= qpos[:, None], s_own, -jnp.inf)
        s = jnp.concatenate([s_sel.reshape(b, h, MOBA_QBLOCK, ksel * MOBA_BLOCK), s_own], axis=-1)
        p = jax.nn.softmax(s, axis=-1).astype(v.dtype)
        p_sel = p[..., :ksel * MOBA_BLOCK].reshape(b, h, MOBA_QBLOCK, ksel, MOBA_BLOCK)
        p_own = p[..., ksel * MOBA_BLOCK:]
        return (jnp.einsum('bhqnk,bhqnkd->bhqd', p_sel, vg)
                + jnp.einsum('bhqk,bhkd->bhqd', p_own, v_own))

    o = lax.map(one, (qs, jnp.arange(nq)))
    return o.transpose(1, 0, 3, 2, 4).reshape(b, t, h * hd)


def _moba_sample(q, k, v, cache_k, cache_v, layer, page_table):
    db, ds, h, hd = q.shape
    scale = hd ** -0.5
    nb = PAST_LEN // MOBA_BLOCK
    ppb = MOBA_BLOCK // PAGE_SIZE
    n_part = PAST_LEN - nb * MOBA_BLOCK
    own_pages = page_table[:, nb * ppb:]
    k_own = jnp.concatenate([cache_k[layer, own_pages].reshape(db, n_part, h, hd).astype(k.dtype), k], axis=1)
    v_own = jnp.concatenate([cache_v[layer, own_pages].reshape(db, n_part, h, hd).astype(v.dtype), v], axis=1)
    qpos = PAST_LEN + jnp.arange(ds)
    kpos = nb * MOBA_BLOCK + jnp.arange(n_part + ds)
    s_own = jnp.einsum('bqhd,bkhd->bhqk', q, k_own, preferred_element_type=F32) * scale
    s_own = jnp.where(kpos[None, :] <= qpos[:, None], s_own, -jnp.inf)
    if nb == 0:
        p_own = jax.nn.softmax(s_own, axis=-1).astype(v.dtype)
        return jnp.einsum('bhqk,bkhd->bqhd', p_own, v_own).reshape(db, ds, h * hd)
    ksel = min(MOBA_TOPK, nb)
    past_pages = page_table[:, :nb * ppb]
    k_past = cache_k[layer, past_pages].reshape(db, nb, MOBA_BLOCK, h, hd)
    kmean = jnp.mean(k_past.astype(F32), axis=2)
    gate = jnp.einsum('bqhd,bnhd->bhqn', q, kmean, preferred_element_type=F32)
    _, idx = lax.top_k(gate, ksel)
    phys = past_pages.reshape(db, nb, ppb)[jnp.arange(db)[:, None, None, None], idx]
    hi = jnp.arange(h)[None, :, None, None, None]
    kg = cache_k[layer, phys, :, hi].reshape(db, h, ds, ksel, MOBA_BLOCK, hd)
    vg = cache_v[layer, phys, :, hi].reshape(db, h, ds, ksel, MOBA_BLOCK, hd)
    s_sel = jnp.einsum('bqhd,bhqnkd->bhqnk', q, kg, preferred_element_type=F32) * scale
    s = jnp.concatenate([s_sel.reshape(db, h, ds, ksel * MOBA_BLOCK), s_own], axis=-1)
    p = jax.nn.softmax(s, axis=-1).astype(v.dtype)
    p_sel = p[..., :ksel * MOBA_BLOCK].reshape(db, h, ds, ksel, MOBA_BLOCK)
    p_own = p[..., ksel * MOBA_BLOCK:]
    o = (jnp.einsum('bhqnk,bhqnkd->bqhd', p_sel, vg.astype(v.dtype))
         + jnp.einsum('bhqk,bkhd->bqhd', p_own, v_own))
    return o.reshape(db, ds, h * hd)


def _mixer_sublayer(x, c, pos, conv_buf, s0, attend, chunk,
                    w_ada, b_ada, w_in, conv_w, a_log, dt_bias, norm_g, w_out, ln_g, ln_b):
    bsz, t, _ = x.shape
    shift, scale, gate = _adaln(c, w_ada, b_ada)
    h = x * (1.0 + scale) + shift
    gq, gk, gv, gz, gb, ga, mq, mk, mv = _split_in(h @ w_in)
    o_gdn, new_buf, s = _gdn_mixer(gq, gk, gv, gz, gb, ga, conv_buf, s0, conv_w, a_log, dt_bias, norm_g, chunk)
    mq = _partial_rope(mq.reshape(bsz, t, MOBA_HEADS, HEAD_DIM), pos)
    mk = _partial_rope(mk.reshape(bsz, t, MOBA_HEADS, HEAD_DIM), pos)
    mv = mv.reshape(bsz, t, MOBA_HEADS, HEAD_DIM)
    o_moba = attend(mq, mk, mv)
    y = jnp.concatenate([o_gdn, o_moba.astype(o_gdn.dtype)], axis=-1) @ w_out
    x = _layer_norm(ALPHA * x + (1.0 + gate) * y, ln_g, ln_b)
    return x, mk, mv, new_buf, s.astype(x.dtype)


def _ffn_sublayer(x, c, conv_buf, w_ada, b_ada, w_up, conv_w, conv_b, w_down, ln_g, ln_b):
    shift, scale, gate = _adaln(c, w_ada, b_ada)
    h = x * (1.0 + scale) + shift
    g, u = jnp.split(h @ w_up, 2, axis=-1)
    g, new_buf = _causal_dwconv(g, conv_buf, conv_w)
    y = (jax.nn.silu(g + conv_b) * u) @ w_down
    return _layer_norm(ALPHA * x + (1.0 + gate) * y, ln_g, ln_b), new_buf


def setup_inputs(seed: int = 0) -> dict:
    key = jax.random.key(seed)
    keys = jax.random.split(key, 40)

    def nrm(i, shape, s):
        return s * jax.random.normal(keys[i], shape, jnp.float32)

    L, D = DEPTH, D_MODEL
    n_pages = PAST_LEN // PAGE_SIZE
    n_used = DEC_BATCH * n_pages
    n_pool = n_used + max(1, n_used // 4)
    page_table = jax.random.permutation(keys[0], n_pool)[:n_used].reshape(DEC_BATCH, n_pages).astype(jnp.int32)
    a_log = jnp.log(jax.random.uniform(keys[1], (L, GDN_HEADS), jnp.float32, 1.0, 16.0))
    dt = jnp.exp(jax.random.uniform(keys[2], (L, GDN_HEADS), jnp.float32, math.log(1e-3), math.log(1e-1)))
    dt_bias = dt + jnp.log(-jnp.expm1(-dt))
    return {
        "x_prompt": nrm(3, (BATCH, SEQ, D), 1.0),
        "x_sample": nrm(4, (DEC_BATCH, DEC_SEQ, D), 1.0),
        "cache_k": nrm(5, (L, n_pool, PAGE_SIZE, MOBA_HEADS, HEAD_DIM), 1.0),
        "cache_v": nrm(6, (L, n_pool, PAGE_SIZE, MOBA_HEADS, HEAD_DIM), 1.0),
        "state_gdn": nrm(7, (L, DEC_BATCH, GDN_HEADS, HEAD_DIM, HEAD_DIM), 0.1),
        "state_gdn_conv": nrm(8, (L, DEC_BATCH, GDN_CONV - 1, 3 * GDN_WIDTH), 1.0),
        "state_ffn_conv": nrm(9, (L, DEC_BATCH, FFN_CONV - 1, D_FF), 1.0),
        "page_table": page_table,
        "c_prompt": nrm(10, (BATCH, D), 1.0),
        "c_sample": nrm(11, (DEC_BATCH, D), 1.0),
        "w_ada_mix": nrm(12, (L, D, 3 * D), 0.1 * D ** -0.5),
        "b_ada_mix": nrm(13, (L, 3 * D), 0.01),
        "w_in": nrm(14, (L, D, N_IN), D ** -0.5),
        "gdn_conv_w": nrm(15, (L, GDN_CONV, 3 * GDN_WIDTH), GDN_CONV ** -0.5),
        "gdn_a_log": a_log,
        "gdn_dt_bias": dt_bias,
        "gdn_norm_g": 1.0 + nrm(16, (L, HEAD_DIM), 0.01),
        "w_out": nrm(17, (L, MIX_WIDTH, D), BETA_INIT * MIX_WIDTH ** -0.5),
        "ln_mix_g": 1.0 + nrm(18, (L, D), 0.01),
        "ln_mix_b": nrm(19, (L, D), 0.01),
        "w_ada_ffn": nrm(20, (L, D, 3 * D), 0.1 * D ** -0.5),
        "b_ada_ffn": nrm(21, (L, 3 * D), 0.01),
        "w_up": nrm(22, (L, D, 2 * D_FF), D ** -0.5),
        "ffn_conv_w": nrm(23, (L, FFN_CONV, D_FF), FFN_CONV ** -0.5),
        "ffn_conv_b": nrm(24, (L, D_FF), 0.01),
        "w_down": nrm(25, (L, D_FF, D), BETA_INIT * D_FF ** -0.5),
        "ln_ffn_g": 1.0 + nrm(26, (L, D), 0.01),
        "ln_ffn_b": nrm(27, (L, D), 0.01),
    }


def reference(x_prompt, x_sample, cache_k, cache_v, state_gdn, state_gdn_conv, state_ffn_conv, page_table,
              c_prompt, c_sample, w_ada_mix, b_ada_mix, w_in, gdn_conv_w, gdn_a_log, gdn_dt_bias, gdn_norm_g,
              w_out, ln_mix_g, ln_mix_b, w_ada_ffn, b_ada_ffn, w_up, ffn_conv_w, ffn_conv_b, w_down,
              ln_ffn_g, ln_ffn_b):
    bsz, seq = x_prompt.shape[0], x_prompt.shape[1]
    dec_seq = x_sample.shape[1]
    pos_p = jnp.arange(seq)
    pos_s = PAST_LEN + jnp.arange(dec_seq)
    xp, xs = x_prompt, x_sample
    kp_l, vp_l, ks_l, vs_l, sp_l, ss_l, gcp_l, gcs_l, fcp_l, fcs_l = ([] for _ in range(10))
    for l in range(DEPTH):
        mix_w = (w_ada_mix[l], b_ada_mix[l], w_in[l], gdn_conv_w[l], gdn_a_log[l], gdn_dt_bias[l],
                 gdn_norm_g[l], w_out[l], ln_mix_g[l], ln_mix_b[l])
        ffn_w = (w_ada_ffn[l], b_ada_ffn[l], w_up[l], ffn_conv_w[l], ffn_conv_b[l], w_down[l],
                 ln_ffn_g[l], ln_ffn_b[l])
        zero_conv = jnp.zeros((bsz, GDN_CONV - 1, 3 * GDN_WIDTH), xp.dtype)
        zero_s = jnp.zeros((bsz, GDN_HEADS, HEAD_DIM, HEAD_DIM), F32)
        xp, kp, vp, gcp, sp = _mixer_sublayer(xp, c_prompt, pos_p, zero_conv, zero_s, _moba_prompt,
                                              GDN_CHUNK, *mix_w)
        xp, fcp = _ffn_sublayer(xp, c_prompt, jnp.zeros((bsz, FFN_CONV - 1, D_FF), xp.dtype), *ffn_w)
        attend_s = functools.partial(_moba_sample, cache_k=cache_k, cache_v=cache_v, layer=l,
                                     page_table=page_table)
        xs, ks, vs, gcs, ss = _mixer_sublayer(xs, c_sample, pos_s, state_gdn_conv[l], state_gdn[l], attend_s,
                                              dec_seq, *mix_w)
        xs, fcs = _ffn_sublayer(xs, c_sample, state_ffn_conv[l], *ffn_w)
        kp_l.append(kp); vp_l.append(vp); ks_l.append(ks); vs_l.append(vs)
        sp_l.append(sp); ss_l.append(ss); gcp_l.append(gcp); gcs_l.append(gcs)
        fcp_l.append(fcp); fcs_l.append(fcs)
    return (xp, xs, jnp.stack(kp_l), jnp.stack(vp_l), jnp.stack(ks_l), jnp.stack(vs_l),
            jnp.stack(sp_l), jnp.stack(ss_l), jnp.stack(gcp_l), jnp.stack(gcs_l),
            jnp.stack(fcp_l), jnp.stack(fcs_l))
```

```python
import functools
import math

import jax
import jax.numpy as jnp
from jax import lax
from jax.experimental import pallas as pl
from jax.experimental.pallas import tpu as pltpu

F32 = jnp.float32
MXU_DTYPE = jnp.bfloat16

HEAD_DIM = 128
LANES = 128
SUBLANES = 8
N_HEADS = 8
WIDTH = N_HEADS * HEAD_DIM
GDN_CONV = 4
FFN_CONV = 3
MOBA_BLOCK = 256
MOBA_TOPK = 3
PAGE_SIZE = 128
ROPE_THETA = 500000.0
ROPE_DIM = HEAD_DIM // 4
LN_EPS = 1e-5
NORM_EPS = 1e-6
GDN_CHUNK = 128
INV_BLOCK = 16
VMEM_LIMIT = 56 * 1024 * 1024
NEG_INF = float("-inf")


def _cparams(*sem):
    return pltpu.CompilerParams(dimension_semantics=sem, vmem_limit_bytes=VMEM_LIMIT)


def _mm(a, b):
    return jnp.dot(a.astype(MXU_DTYPE), b.astype(MXU_DTYPE), preferred_element_type=F32)


def _mm_nt(a, b):
    return lax.dot_general(a.astype(MXU_DTYPE), b.astype(MXU_DTYPE), (((1,), (1,)), ((), ())),
                           preferred_element_type=F32)


def _mm_tn(a, b):
    return lax.dot_general(a.astype(MXU_DTYPE), b.astype(MXU_DTYPE), (((0,), (0,)), ((), ())),
                           preferred_element_type=F32)


def _silu(x):
    return x * jax.nn.sigmoid(x)


def _layer_norm_rows(r, g, b):
    mu = jnp.mean(r, axis=-1, keepdims=True)
    d = r - mu
    var = jnp.mean(d * d, axis=-1, keepdims=True)
    return d * lax.rsqrt(var + LN_EPS) * g + b


def _adaln_kernel(c_ref, w_ref, b_ref, o_ref):
    o_ref[...] = _mm(c_ref[...], w_ref[...]) + b_ref[...]


def _adaln(c, w, b):
    n_layers, d, n = w.shape
    rows = c.shape[0]
    tn = 1024
    return pl.pallas_call(
        _adaln_kernel,
        out_shape=jax.ShapeDtypeStruct((n_layers, rows, n), F32),
        grid=(n_layers, n // tn),
        in_specs=[pl.BlockSpec((rows, d), lambda l, j: (0, 0)),
                  pl.BlockSpec((None, d, tn), lambda l, j: (l, 0, j)),
                  pl.BlockSpec((None, 1, tn), lambda l, j: (l, 0, j))],
        out_specs=pl.BlockSpec((None, rows, tn), lambda l, j: (l, 0, j)),
        compiler_params=_cparams("arbitrary", "arbitrary"),
        name="adaln",
    )(c, w, b.reshape(n_layers, 1, n))


def _rope_store(y, c, s1, s2, o_ref):
    half = ROPE_DIM // 2
    for h in range(N_HEADS):
        yh = y[:, h * HEAD_DIM:(h + 1) * HEAD_DIM]
        o_ref[:, h * HEAD_DIM:(h + 1) * HEAD_DIM] = (
            yh * c + pltpu.roll(yh, HEAD_DIM - half, 1) * s1 + pltpu.roll(yh, half, 1) * s2)


def _in_proj_kernel(x_ref, sc_ref, sh_ref, w_ref, wba_ref, rc_ref, rs1_ref, rs2_ref,
                    g_ref, ba_ref, mq_ref, mk_ref, mv_ref, h_ref):
    j = pl.program_id(1)

    @pl.when(j == 0)
    def _():
        h_ref[...] = (x_ref[...] * (1.0 + sc_ref[...]) + sh_ref[...]).astype(h_ref.dtype)
        ba_ref[...] = jnp.dot(h_ref[...], wba_ref[...], preferred_element_type=F32)

    def proj():
        return jnp.dot(h_ref[...], w_ref[...], preferred_element_type=F32)

    @pl.when(j < 4)
    def _():
        g_ref[...] = proj()

    @pl.when(j == 4)
    def _():
        _rope_store(proj(), rc_ref[...], rs1_ref[...], rs2_ref[...], mq_ref)

    @pl.when(j == 5)
    def _():
        _rope_store(proj(), rc_ref[...], rs1_ref[...], rs2_ref[...], mk_ref)

    @pl.when(j == 6)
    def _():
        mv_ref[...] = proj()


def _in_proj(x, scale, shift, w_all, w_ba, rope, tm, mod_index, rope_index):
    m, d = x.shape
    r = scale.shape[1]
    mod_spec = pl.BlockSpec((None, r, d), lambda i, j: (mod_index(i), 0, 0))
    rope_spec = pl.BlockSpec((tm, LANES), lambda i, j: (rope_index(i), 0))
    row_spec = pl.BlockSpec((tm, WIDTH), lambda i, j: (i, 0))
    return pl.pallas_call(
        _in_proj_kernel,
        out_shape=(jax.ShapeDtypeStruct((m, 4 * WIDTH), F32),
                   jax.ShapeDtypeStruct((m, LANES), F32),
                   jax.ShapeDtypeStruct((m, WIDTH), F32),
                   jax.ShapeDtypeStruct((m, WIDTH), F32),
                   jax.ShapeDtypeStruct((m, WIDTH), F32)),
        grid=(m // tm, 7),
        in_specs=[pl.BlockSpec((tm, d), lambda i, j: (i, 0)), mod_spec, mod_spec,
                  pl.BlockSpec((d, WIDTH), lambda i, j: (0, j)),
                  pl.BlockSpec((d, LANES), lambda i, j: (0, 0)),
                  rope_spec, rope_spec, rope_spec],
        out_specs=(pl.BlockSpec((tm, WIDTH), lambda i, j: (i, jnp.minimum(j, 3))),
                   pl.BlockSpec((tm, LANES), lambda i, j: (i, 0)),
                   row_spec, row_spec, row_spec),
        scratch_shapes=[pltpu.VMEM((tm, d), MXU_DTYPE)],
        compiler_params=_cparams("arbitrary", "arbitrary"),
        name="in_proj",
    )(x, scale, shift, w_all, w_ba, *rope)


def _unit_lower_inverse(a_low, eye, same_blk):
    n = -a_low
    d = jnp.where(same_blk, n, 0.0)
    o = n - d
    x = eye + d
    p = d
    for _ in range(int(math.log2(INV_BLOCK)) - 1):
        p = _mm(p, p)
        x = x + _mm(x, p)
    y = _mm(x, o)
    z = eye + y
    p = y
    for _ in range(int(math.log2(GDN_CHUNK // INV_BLOCK)) - 1):
        p = _mm(p, p)
        z = z + _mm(z, p)
    return z, x


def _gdn_chunk(q, k, v, beta_c, eg_c, ek_c, gl, gc_col, gc_row, s, masks):
    eye, same_blk, tri_incl, tri_strict = masks
    c = q.shape[0]
    kb = k * beta_c
    p = _mm_nt(jnp.concatenate([q, kb], axis=0), k)
    decay = jnp.exp(jnp.where(tri_incl, gc_col - gc_row, NEG_INF))
    qk = p[:c] * decay
    a_low = jnp.where(tri_strict, p[c:] * decay, 0.0)
    z, x = _unit_lower_inverse(a_low, eye, same_blk)
    rhs = jnp.concatenate([v * beta_c, kb * eg_c], axis=1)
    sol = _mm(z, _mm(x, rhs))
    u, w = sol[:, :HEAD_DIM], sol[:, HEAD_DIM:]
    v_new = u - _mm(w, s)
    o = _mm(q * eg_c, s) + _mm(qk, v_new)
    s_new = s * gl + _mm_tn(k * ek_c, v_new)
    return o, s_new


def _gdn_kernel(x_ref, ba_ref, cw_ref, alog_ref, dtb_ref, ng_ref, *rest, tt, n_valid, from_state):
    if from_state:
        cs_ref, s0_ref, o_ref, sout_ref, xbuf, zbuf, babuf, s_scr = rest
    else:
        o_ref, sout_ref, xbuf, s_scr = rest
    t = pl.program_id(1)
    nk = GDN_CONV - 1
    c = GDN_CHUNK

    if from_state:
        xbuf[...] = jnp.zeros(xbuf.shape, F32)
        xbuf[SUBLANES - nk:SUBLANES, :] = cs_ref[...]
        xbuf[SUBLANES:SUBLANES + n_valid, :] = x_ref[:, 0:3 * WIDTH]
        zbuf[...] = jnp.zeros(zbuf.shape, F32)
        zbuf[0:n_valid, :] = x_ref[:, 3 * WIDTH:4 * WIDTH]
        babuf[...] = jnp.zeros(babuf.shape, F32)
        babuf[0:n_valid, :] = ba_ref[...]
        s_scr[...] = s0_ref[...]
        ba = babuf[...]
    else:
        @pl.when(t == 0)
        def _():
            xbuf[0:SUBLANES, :] = jnp.zeros((SUBLANES, 3 * WIDTH), F32)
            s_scr[...] = jnp.zeros(s_scr.shape, F32)

        @pl.when(t > 0)
        def _():
            xbuf[0:SUBLANES, :] = xbuf[tt:tt + SUBLANES, :]

        xbuf[SUBLANES:SUBLANES + tt, :] = x_ref[:, 0:3 * WIDTH]
        ba = ba_ref[...]

    beta_all = jax.nn.sigmoid(ba)
    a_in = ba + dtb_ref[...]
    softplus = jnp.maximum(a_in, 0.0) + jnp.log1p(jnp.exp(-jnp.abs(a_in)))
    g_all = -jnp.exp(alog_ref[...]) * softplus
    if n_valid < tt:
        live = lax.broadcasted_iota(jnp.int32, (tt, LANES), 0) < n_valid
        beta_all = jnp.where(live, beta_all, 0.0)
        g_all = jnp.where(live, g_all, 0.0)

    row = lax.broadcasted_iota(jnp.int32, (c, c), 0)
    col = lax.broadcasted_iota(jnp.int32, (c, c), 1)
    tri_incl = row >= col
    tri_strict = row > col
    masks = (jnp.where(row == col, 1.0, 0.0).astype(F32), (row // INV_BLOCK) == (col // INV_BLOCK),
             tri_incl, tri_strict)
    ltri = jnp.where(tri_incl, 1.0, 0.0).astype(F32)

    def conv_act(col0):
        acc = xbuf[SUBLANES - nk:SUBLANES - nk + tt, col0:col0 + HEAD_DIM] * cw_ref[0:1, col0:col0 + HEAD_DIM]
        for i in range(1, GDN_CONV):
            acc = acc + (xbuf[SUBLANES - nk + i:SUBLANES - nk + i + tt, col0:col0 + HEAD_DIM]
                         * cw_ref[i:i + 1, col0:col0 + HEAD_DIM])
        return _silu(acc)

    def l2n(a):
        return a * lax.rsqrt(jnp.sum(a * a, axis=-1, keepdims=True) + NORM_EPS)

    chunk_decay = []
    for ci in range(tt // c):
        gcum = jnp.dot(ltri, g_all[ci * c:(ci + 1) * c, :], precision=lax.Precision.HIGHEST,
                       preferred_element_type=F32)
        chunk_decay.append((gcum, gcum.T, jnp.exp(gcum), jnp.exp(gcum[c - 1:c, :] - gcum)))

    for h in range(N_HEADS):
        hs = slice(h * HEAD_DIM, (h + 1) * HEAD_DIM)
        q_t = l2n(conv_act(h * HEAD_DIM)) * (HEAD_DIM ** -0.5)
        k_t = l2n(conv_act(WIDTH + h * HEAD_DIM))
        v_t = conv_act(2 * WIDTH + h * HEAD_DIM)
        gl_lane = N_HEADS + h
        s = s_scr[h]
        for ci in range(tt // c):
            r0 = ci * c
            gcum, gcum_t, eg, ek = chunk_decay[ci]
            o, s = _gdn_chunk(
                q_t[r0:r0 + c], k_t[r0:r0 + c], v_t[r0:r0 + c],
                beta_all[r0:r0 + c, h:h + 1], eg[:, gl_lane:gl_lane + 1], ek[:, gl_lane:gl_lane + 1],
                eg[c - 1:c, gl_lane:gl_lane + 1], gcum[:, gl_lane:gl_lane + 1], gcum_t[gl_lane:gl_lane + 1, :],
                s, masks)
            if from_state:
                zh = zbuf[r0:r0 + c, hs]
            else:
                zh = x_ref[r0:r0 + c, 3 * WIDTH + h * HEAD_DIM:3 * WIDTH + (h + 1) * HEAD_DIM]
            on = o * lax.rsqrt(jnp.mean(o * o, axis=-1, keepdims=True) + NORM_EPS) * ng_ref[...] * _silu(zh)
            if from_state:
                o_ref[:, hs] = on[0:n_valid].astype(o_ref.dtype)
            else:
                o_ref[r0:r0 + c, hs] = on.astype(o_ref.dtype)
        s_scr[h] = s

    sout_ref[...] = s_scr[...]


def _gdn_prompt(g, ba, conv_w, alog_pad, dtb_pad, norm_g, batch, seq, tt):
    m = g.shape[0]
    nt = seq // tt
    kern = functools.partial(_gdn_kernel, tt=tt, n_valid=tt, from_state=False)
    small = lambda shape: pl.BlockSpec(shape, lambda b, t: (0, 0))
    return pl.pallas_call(
        kern,
        out_shape=(jax.ShapeDtypeStruct((m, WIDTH), MXU_DTYPE),
                   jax.ShapeDtypeStruct((batch, N_HEADS, HEAD_DIM, HEAD_DIM), F32)),
        grid=(batch, nt),
        in_specs=[pl.BlockSpec((tt, 4 * WIDTH), lambda b, t: (b * nt + t, 0)),
                  pl.BlockSpec((tt, LANES), lambda b, t: (b * nt + t, 0)),
                  small((GDN_CONV, 3 * WIDTH)), small((1, LANES)), small((1, LANES)), small((1, HEAD_DIM))],
        out_specs=(pl.BlockSpec((tt, WIDTH), lambda b, t: (b * nt + t, 0)),
                   pl.BlockSpec((None, N_HEADS, HEAD_DIM, HEAD_DIM), lambda b, t: (b, 0, 0, 0))),
        scratch_shapes=[pltpu.VMEM((tt + SUBLANES, 3 * WIDTH), F32),
                        pltpu.VMEM((N_HEADS, HEAD_DIM, HEAD_DIM), F32)],
        compiler_params=_cparams("arbitrary", "arbitrary"),
        name="gdn_prompt",
    )(g, ba, conv_w, alog_pad, dtb_pad, norm_g)


def _gdn_sample(g, ba, conv_state, s0, conv_w, alog_pad, dtb_pad, norm_g, batch, dec_seq):
    tt = GDN_CHUNK
    kern = functools.partial(_gdn_kernel, tt=tt, n_valid=dec_seq, from_state=True)
    small = lambda shape: pl.BlockSpec(shape, lambda b, t: (0, 0))
    return pl.pallas_call(
        kern,
        out_shape=(jax.ShapeDtypeStruct((batch, dec_seq, WIDTH), F32),
                   jax.ShapeDtypeStruct((batch, N_HEADS, HEAD_DIM, HEAD_DIM), F32)),
        grid=(batch, 1),
        in_specs=[pl.BlockSpec((None, dec_seq, 4 * WIDTH), lambda b, t: (b, 0, 0)),
                  pl.BlockSpec((None, dec_seq, LANES), lambda b, t: (b, 0, 0)),
                  small((GDN_CONV, 3 * WIDTH)), small((1, LANES)), small((1, LANES)), small((1, HEAD_DIM)),
                  pl.BlockSpec((None, GDN_CONV - 1, 3 * WIDTH), lambda b, t: (b, 0, 0)),
                  pl.BlockSpec((None, N_HEADS, HEAD_DIM, HEAD_DIM), lambda b, t: (b, 0, 0, 0))],
        out_specs=(pl.BlockSpec((None, dec_seq, WIDTH), lambda b, t: (b, 0, 0)),
                   pl.BlockSpec((None, N_HEADS, HEAD_DIM, HEAD_DIM), lambda b, t: (b, 0, 0, 0))),
        scratch_shapes=[pltpu.VMEM((tt + SUBLANES, 3 * WIDTH), F32),
                        pltpu.VMEM((tt, WIDTH), F32),
                        pltpu.VMEM((tt, LANES), F32),
                        pltpu.VMEM((N_HEADS, HEAD_DIM, HEAD_DIM), F32)],
        compiler_params=_cparams("arbitrary", "arbitrary"),
        name="gdn_sample",
    )(g, ba, conv_w, alog_pad, dtb_pad, norm_g, conv_state, s0)


def _topk_select(gate, n_valid, idx, axis):
    n = gate.shape[axis]
    cnt = jnp.zeros(gate.shape, F32)
    for j in range(n):
        cand = lax.slice_in_dim(gate, j, j + 1, axis=axis)
        beats = jnp.where(cand > gate, 1.0, jnp.where((cand == gate) & (j < idx), 1.0, 0.0))
        cnt = cnt + jnp.where(j < n_valid, beats, 0.0)
    return (cnt < MOBA_TOPK) & (idx < n_valid)


def _moba_prompt_kernel(q_ref, k_ref, v_ref, o_ref, k16, vt16, kmean, sel_scr, *, nb):
    i = pl.program_id(2)
    blk = MOBA_BLOCK
    scale = HEAD_DIM ** -0.5

    @pl.when(i == 0)
    def _():
        for j in range(nb):
            kj = k_ref[j * blk:(j + 1) * blk, :]
            k16[j] = kj.astype(MXU_DTYPE)
            kmean[j:j + 1, :] = jnp.mean(kj, axis=0, keepdims=True)
            vt16[j] = v_ref[j * blk:(j + 1) * blk, :].T.astype(MXU_DTYPE)

    qt = q_ref[...].T
    gate = jnp.dot(kmean[...], qt, precision=lax.Precision.HIGHEST, preferred_element_type=F32)
    blk_idx = lax.broadcasted_iota(jnp.int32, gate.shape, 0)
    sel = _topk_select(gate, i, blk_idx, 0)
    sel_scr[...] = jnp.where(sel, 1.0, 0.0)
    qt16 = qt.astype(MXU_DTYPE)

    kpos = lax.broadcasted_iota(jnp.int32, (blk, blk), 0)
    qpos = lax.broadcasted_iota(jnp.int32, (blk, blk), 1)
    st = jnp.dot(k16[i], qt16, preferred_element_type=F32) * scale
    st = jnp.where(kpos <= qpos, st, NEG_INF)
    m0 = jnp.max(st, axis=0, keepdims=True)
    p = jnp.exp(st - m0)
    l0 = jnp.sum(p, axis=0, keepdims=True)
    acc0 = jnp.dot(vt16[i], p.astype(MXU_DTYPE), preferred_element_type=F32)

    def body(j, carry):
        m, l, acc = carry
        sj = jnp.dot(k16[j], qt16, preferred_element_type=F32) * scale
        sj = jnp.where(sel_scr[pl.ds(j, 1), :] > 0.5, sj, NEG_INF)
        m_new = jnp.maximum(m, jnp.max(sj, axis=0, keepdims=True))
        alpha = jnp.exp(m - m_new)
        pj = jnp.exp(sj - m_new)
        l = alpha * l + jnp.sum(pj, axis=0, keepdims=True)
        acc = alpha * acc + jnp.dot(vt16[j], pj.astype(MXU_DTYPE), preferred_element_type=F32)
        return m_new, l, acc

    _, l, acc = lax.fori_loop(0, i, body, (m0, l0, acc0))
    o_ref[...] = (acc / l).T.astype(o_ref.dtype)


def _moba_prompt(mq, mk, mv, batch, seq):
    m = mq.shape[0]
    nb = seq // MOBA_BLOCK
    kern = functools.partial(_moba_prompt_kernel, nb=nb)
    kv_spec = pl.BlockSpec((seq, HEAD_DIM), lambda b, h, i: (b, h))
    return pl.pallas_call(
        kern,
        out_shape=jax.ShapeDtypeStruct((m, WIDTH), MXU_DTYPE),
        grid=(batch, N_HEADS, nb),
        in_specs=[pl.BlockSpec((MOBA_BLOCK, HEAD_DIM), lambda b, h, i: (b * nb + i, h)), kv_spec, kv_spec],
        out_specs=pl.BlockSpec((MOBA_BLOCK, HEAD_DIM), lambda b, h, i: (b * nb + i, h)),
        scratch_shapes=[pltpu.VMEM((nb, MOBA_BLOCK, HEAD_DIM), MXU_DTYPE),
                        pltpu.VMEM((nb, HEAD_DIM, MOBA_BLOCK), MXU_DTYPE),
                        pltpu.VMEM((nb, HEAD_DIM), F32),
                        pltpu.VMEM((nb, MOBA_BLOCK), F32)],
        compiler_params=_cparams("arbitrary", "arbitrary", "arbitrary"),
        name="moba_prompt",
    )(mq, mk, mv)


QPAD = SUBLANES


def _moba_partial_kernel(pt_ref, q_ref, k0_ref, k1_ref, v0_ref, v1_ref, kmean_ref, pm_ref, pl_ref, pacc_ref):
    n = pl.program_id(1)
    scale = HEAD_DIM ** -0.5
    k0, k1 = k0_ref[...], k1_ref[...]
    kmean_ref[pl.ds(n, 1), :] = (jnp.sum(k0, axis=0, keepdims=True) + jnp.sum(k1, axis=0, keepdims=True)) * (1.0 / MOBA_BLOCK)
    q8 = q_ref[...]
    for h in range(N_HEADS):
        hs = slice(h * HEAD_DIM, (h + 1) * HEAD_DIM)
        kh = jnp.concatenate([k0[:, hs], k1[:, hs]], axis=0)
        vh = jnp.concatenate([v0_ref[:, hs], v1_ref[:, hs]], axis=0)
        s = _mm_nt(q8[:, hs], kh) * scale
        mx = jnp.max(s, axis=-1, keepdims=True)
        p = jnp.exp(s - mx)
        pm_ref[h] = jnp.broadcast_to(mx, (QPAD, LANES))
        pl_ref[h] = jnp.broadcast_to(jnp.sum(p, axis=-1, keepdims=True), (QPAD, LANES))
        pacc_ref[h] = _mm(p, vh)


def _moba_partials(page_table_flat, q, cache_k, cache_v, layer, batch, nb, n_pages):
    ppb = MOBA_BLOCK // PAGE_SIZE
    assert ppb == 2
    kern = _moba_partial_kernel

    def page_spec(which):
        return pl.BlockSpec((None, None, PAGE_SIZE, WIDTH),
                            lambda b, n, pt: (layer, pt[b * n_pages + ppb * n + which], 0, 0))

    part_shape = jax.ShapeDtypeStruct((batch, nb, N_HEADS, QPAD, LANES), F32)
    part_spec = pl.BlockSpec((None, None, N_HEADS, QPAD, LANES), lambda b, n, pt: (b, n, 0, 0, 0))
    return pl.pallas_call(
        kern,
        out_shape=(jax.ShapeDtypeStruct((batch, nb, WIDTH), F32), part_shape, part_shape, part_shape),
        grid_spec=pltpu.PrefetchScalarGridSpec(
            num_scalar_prefetch=1,
            grid=(batch, nb),
            in_specs=[pl.BlockSpec((None, QPAD, WIDTH), lambda b, n, pt: (b, 0, 0)),
                      page_spec(0), page_spec(1), page_spec(0), page_spec(1)],
            out_specs=(pl.BlockSpec((None, nb, WIDTH), lambda b, n, pt: (b, 0, 0)),
                       part_spec, part_spec, part_spec)),
        compiler_params=_cparams("arbitrary", "arbitrary"),
        name="moba_sample_partials",
    )(page_table_flat, q, cache_k, cache_k, cache_v, cache_v)


def _moba_merge_kernel(q_ref, k_ref, v_ref, kmean_ref, pm_ref, pl_ref, pacc_ref, o_ref, *, dec_seq, nb):
    scale = HEAD_DIM ** -0.5
    q8, k8, v8 = q_ref[...], k_ref[...], v_ref[...]
    qi = lax.broadcasted_iota(jnp.int32, (QPAD, QPAD), 0)
    kj = lax.broadcasted_iota(jnp.int32, (QPAD, QPAD), 1)
    own_ok = (kj <= qi) & (kj < dec_seq)
    blk_idx = lax.broadcasted_iota(jnp.int32, (QPAD, nb), 1)
    for h in range(N_HEADS):
        hs = slice(h * HEAD_DIM, (h + 1) * HEAD_DIM)
        gate = lax.dot_general(q8[:, hs], kmean_ref[:, hs], (((1,), (1,)), ((), ())),
                               precision=lax.Precision.HIGHEST, preferred_element_type=F32)
        sel = jnp.where(_topk_select(gate, nb, blk_idx, 1), 1.0, 0.0)
        s_own = jnp.where(own_ok, _mm_nt(q8[:, hs], k8[:, hs]) * scale, NEG_INF)
        m_tot = jnp.broadcast_to(jnp.max(s_own, axis=-1, keepdims=True), (QPAD, LANES))
        for n in range(nb):
            m_tot = jnp.maximum(m_tot, jnp.where(sel[:, n:n + 1] > 0.5, pm_ref[n, h], NEG_INF))
        p_own = jnp.exp(s_own - m_tot[:, 0:1])
        l_tot = jnp.broadcast_to(jnp.sum(p_own, axis=-1, keepdims=True), (QPAD, LANES))
        acc = jnp.zeros((QPAD, HEAD_DIM), F32)
        for j in range(dec_seq):
            acc = acc + p_own[:, j:j + 1] * v8[j:j + 1, hs]
        for n in range(nb):
            w = jnp.where(sel[:, n:n + 1] > 0.5, jnp.exp(pm_ref[n, h] - m_tot), 0.0)
            l_tot = l_tot + w * pl_ref[n, h]
            acc = acc + w * pacc_ref[n, h]
        o_ref[:, hs] = (acc / l_tot)[0:dec_seq].astype(o_ref.dtype)


def _moba_merge(q, k, v, kmean, pm, plsum, pacc, batch, dec_seq, nb):
    kern = functools.partial(_moba_merge_kernel, dec_seq=dec_seq, nb=nb)
    row_spec = pl.BlockSpec((None, QPAD, WIDTH), lambda b: (b, 0, 0))
    part_spec = pl.BlockSpec((None, nb, N_HEADS, QPAD, LANES), lambda b: (b, 0, 0, 0, 0))
    return pl.pallas_call(
        kern,
        out_shape=jax.ShapeDtypeStruct((batch, dec_seq, WIDTH), F32),
        grid=(batch,),
        in_specs=[row_spec, row_spec, row_spec, pl.BlockSpec((None, nb, WIDTH), lambda b: (b, 0, 0)),
                  part_spec, part_spec, part_spec],
        out_specs=pl.BlockSpec((None, dec_seq, WIDTH), lambda b: (b, 0, 0)),
        compiler_params=_cparams("arbitrary"),
        name="moba_sample_merge",
    )(q, k, v, kmean, pm, plsum, pacc)


def _out_ln_kernel(og_ref, om_ref, x_ref, gate_ref, w1_ref, w2_ref, lng_ref, lnb_ref, o_ref, *, alpha):
    y = _mm(og_ref[...], w1_ref[...]) + _mm(om_ref[...], w2_ref[...])
    r = alpha * x_ref[...] + (1.0 + gate_ref[...]) * y
    o_ref[...] = _layer_norm_rows(r, lng_ref[...], lnb_ref[...])


def _out_ln(o_gdn, o_moba, x, gate, w_out, ln_g, ln_b, tm, mod_index, alpha):
    m, d = x.shape
    r = gate.shape[1]
    vec = pl.BlockSpec((1, d), lambda i: (0, 0))
    return pl.pallas_call(
        functools.partial(_out_ln_kernel, alpha=alpha),
        out_shape=jax.ShapeDtypeStruct((m, d), F32),
        grid=(m // tm,),
        in_specs=[pl.BlockSpec((tm, WIDTH), lambda i: (i, 0)), pl.BlockSpec((tm, WIDTH), lambda i: (i, 0)),
                  pl.BlockSpec((tm, d), lambda i: (i, 0)),
                  pl.BlockSpec((None, r, d), lambda i: (mod_index(i), 0, 0)),
                  pl.BlockSpec((WIDTH, d), lambda i: (0, 0)), pl.BlockSpec((WIDTH, d), lambda i: (1, 0)),
                  vec, vec],
        out_specs=pl.BlockSpec((tm, d), lambda i: (i, 0)),
        compiler_params=_cparams("arbitrary"),
        name="out_proj_ln",
    )(o_gdn, o_moba, x, gate, w_out, w_out, ln_g, ln_b)


def _ffn_kernel(x_ref, sc_ref, sh_ref, gate_ref, wg_ref, wu_ref, wd_ref, cw_ref, cb_ref, lng_ref, lnb_ref, *rest,
                tm, n_chunks, tiles_per_seq, dec_seq, alpha):
    sample = dec_seq is not None
    if sample:
        p1_ref, p2_ref, o_ref, st_ref, h_ref, acc_ref, gbuf = rest
    else:
        o_ref, st_ref, h_ref, acc_ref, gbuf, carry = rest
    i = pl.program_id(0)
    c = pl.program_id(1)

    @pl.when(c == 0)
    def _():
        h_ref[...] = (x_ref[...] * (1.0 + sc_ref[...]) + sh_ref[...]).astype(h_ref.dtype)
        acc_ref[...] = jnp.zeros(acc_ref.shape, F32)

    g = jnp.dot(h_ref[...], wg_ref[...], preferred_element_type=F32)
    u = jnp.dot(h_ref[...], wu_ref[...], preferred_element_type=F32)
    gbuf[SUBLANES:SUBLANES + tm, :] = g
    if sample:
        gbuf[0:SUBLANES, :] = jnp.zeros((SUBLANES, g.shape[1]), F32)
        pos = lax.broadcasted_iota(jnp.int32, g.shape, 0) % dec_seq
        prev1 = jnp.where(pos >= 1, gbuf[SUBLANES - 1:SUBLANES - 1 + tm, :], p1_ref[...])
        prev2 = jnp.where(pos >= 2, gbuf[SUBLANES - 2:SUBLANES - 2 + tm, :], p2_ref[...])
        st_ref[...] = g
    else:
        first = (i % tiles_per_seq) == 0

        @pl.when(first)
        def _():
            gbuf[0:SUBLANES, :] = jnp.zeros((SUBLANES, g.shape[1]), F32)

        @pl.when(jnp.logical_not(first))
        def _():
            gbuf[0:SUBLANES, :] = carry[c]

        carry[c] = g[tm - SUBLANES:tm, :]
        prev1 = gbuf[SUBLANES - 1:SUBLANES - 1 + tm, :]
        prev2 = gbuf[SUBLANES - 2:SUBLANES - 2 + tm, :]
        st_ref[...] = g[tm - (FFN_CONV - 1):tm, :]
    conv = prev2 * cw_ref[0:1, :] + prev1 * cw_ref[1:2, :] + g * cw_ref[2:3, :]
    act = _silu(conv + cb_ref[...]) * u
    acc_ref[...] += jnp.dot(act.astype(MXU_DTYPE), wd_ref[...], preferred_element_type=F32)

    @pl.when(c == n_chunks - 1)
    def _():
        r = alpha * x_ref[...] + (1.0 + gate_ref[...]) * acc_ref[...]
        o_ref[...] = _layer_norm_rows(r, lng_ref[...], lnb_ref[...])


def _ffn(x, scale, shift, gate, w_up, w_down, conv_w, conv_b, ln_g, ln_b, tm, tc, mod_index, alpha,
         tiles_per_seq=None, batch=None, prev_rows=None, dec_seq=None):
    m, d = x.shape
    d_ff = w_down.shape[0]
    n_chunks = d_ff // tc
    r = scale.shape[1]
    sample = prev_rows is not None
    mod_spec = pl.BlockSpec((None, r, d), lambda i, c: (mod_index(i), 0, 0))
    vec = pl.BlockSpec((1, d), lambda i, c: (0, 0))
    in_specs = [pl.BlockSpec((tm, d), lambda i, c: (i, 0)), mod_spec, mod_spec, mod_spec,
                pl.BlockSpec((d, tc), lambda i, c: (0, c)),
                pl.BlockSpec((d, tc), lambda i, c: (0, n_chunks + c)),
                pl.BlockSpec((tc, d), lambda i, c: (c, 0)),
                pl.BlockSpec((FFN_CONV, tc), lambda i, c: (0, c)),
                pl.BlockSpec((1, tc), lambda i, c: (0, c)),
                vec, vec]
    args = [x, scale, shift, gate, w_up, w_up, w_down, conv_w, conv_b, ln_g, ln_b]
    scratch = [pltpu.VMEM((tm, d), MXU_DTYPE), pltpu.VMEM((tm, d), F32), pltpu.VMEM((tm + SUBLANES, tc), F32)]
    if sample:
        in_specs += [pl.BlockSpec((tm, tc), lambda i, c: (i, c))] * 2
        args += list(prev_rows)
        st_shape = jax.ShapeDtypeStruct((m, d_ff), F32)
        st_spec = pl.BlockSpec((tm, tc), lambda i, c: (i, c))
    else:
        st_shape = jax.ShapeDtypeStruct((m // tm, FFN_CONV - 1, d_ff), F32)
        st_spec = pl.BlockSpec((None, FFN_CONV - 1, tc), lambda i, c: (i, 0, c))
        scratch.append(pltpu.VMEM((n_chunks, SUBLANES, tc), F32))
    kern = functools.partial(_ffn_kernel, tm=tm, n_chunks=n_chunks, tiles_per_seq=tiles_per_seq,
                             dec_seq=dec_seq if sample else None, alpha=alpha)
    return pl.pallas_call(
        kern,
        out_shape=(jax.ShapeDtypeStruct((m, d), F32), st_shape),
        grid=(m // tm, n_chunks),
        in_specs=in_specs,
        out_specs=(pl.BlockSpec((tm, d), lambda i, c: (i, 0)), st_spec),
        scratch_shapes=scratch,
        compiler_params=_cparams("arbitrary", "arbitrary"),
        name="ffn_sample" if sample else "ffn_prompt",
    )(*args)


def _rope_tables(pos):
    half = ROPE_DIM // 2
    inv_freq = ROPE_THETA ** (-jnp.arange(half, dtype=F32) / half)
    ang = pos.astype(F32)[:, None] * inv_freq[None, :]
    cos, sin = jnp.cos(ang), jnp.sin(ang)
    n = pos.shape[0]
    ones = jnp.ones((n, HEAD_DIM - ROPE_DIM), F32)
    zeros = jnp.zeros((n, HEAD_DIM - ROPE_DIM), F32)
    zh = jnp.zeros((n, half), F32)
    return (jnp.concatenate([cos, cos, ones], axis=1),
            jnp.concatenate([-sin, zh, zeros], axis=1),
            jnp.concatenate([zh, sin, zeros], axis=1))


def _forward(x_prompt, x_sample, cache_k, cache_v, state_gdn, state_gdn_conv, state_ffn_conv, page_table,
             c_prompt, c_sample, w_ada_mix, b_ada_mix, w_in, gdn_conv_w, gdn_a_log, gdn_dt_bias, gdn_norm_g,
             w_out, ln_mix_g, ln_mix_b, w_ada_ffn, b_ada_ffn, w_up, ffn_conv_w, ffn_conv_b, w_down,
             ln_ffn_g, ln_ffn_b, *, past_len, tm_prompt, tc_ffn, tt_gdn):
    bsz, seq, d = x_prompt.shape
    dbsz, dseq, _ = x_sample.shape
    depth = w_in.shape[0]
    d_ff = w_down.shape[1]
    alpha = (2 * depth) ** 0.25
    n_pool = cache_k.shape[1]
    n_pages = page_table.shape[1]
    nb_past = past_len // MOBA_BLOCK
    assert d == 2 * WIDTH and past_len % MOBA_BLOCK == 0 and nb_past >= MOBA_TOPK
    assert n_pages * PAGE_SIZE == past_len and seq % MOBA_BLOCK == 0
    mp, ms = bsz * seq, dbsz * dseq
    tps = seq // tm_prompt

    gw, nh = WIDTH, N_HEADS
    w_main = jnp.concatenate([w_in[:, :, :4 * gw], w_in[:, :, 4 * gw + 2 * nh:]], axis=2).astype(MXU_DTYPE)
    w_ba = jnp.pad(w_in[:, :, 4 * gw:4 * gw + 2 * nh], ((0, 0), (0, 0), (0, LANES - 2 * nh))).astype(MXU_DTYPE)
    w_out16, w_up16, w_down16 = (w.astype(MXU_DTYPE) for w in (w_out, w_up, w_down))
    alog_pad = jnp.pad(gdn_a_log, ((0, 0), (nh, LANES - 2 * nh)))[:, None, :]
    dtb_pad = jnp.pad(gdn_dt_bias, ((0, 0), (nh, LANES - 2 * nh)))[:, None, :]

    c_all = jnp.concatenate([c_prompt, c_sample], axis=0)
    mod_mix = _adaln(c_all, w_ada_mix, b_ada_mix)
    mod_ffn = _adaln(c_all, w_ada_ffn, b_ada_ffn)

    def mods(mod, l):
        out_p, out_s = [], []
        for part in range(3):
            v = mod[l, :, part * d:(part + 1) * d]
            out_p.append(v[:bsz, None, :])
            out_s.append(jnp.repeat(v[bsz:], dseq, axis=0)[None])
        return out_p, out_s

    rope_p = _rope_tables(jnp.arange(seq))
    rope_s = _rope_tables(jnp.tile(past_len + jnp.arange(dseq), dbsz))
    cache_k2 = cache_k.reshape(depth, n_pool, PAGE_SIZE, WIDTH)
    cache_v2 = cache_v.reshape(depth, n_pool, PAGE_SIZE, WIDTH)
    pt_flat = page_table.reshape(-1)

    xp = x_prompt.reshape(mp, d)
    xs = x_sample.reshape(ms, d)
    p_idx = lambda i: i // tps
    zero_idx = lambda i: 0
    outs = [[] for _ in range(10)]
    for l in range(depth):
        (shift_p, scale_p, gate_p), (shift_s, scale_s, gate_s) = mods(mod_mix, l)
        (fshift_p, fscale_p, fgate_p), (fshift_s, fscale_s, fgate_s) = mods(mod_ffn, l)
        lng, lnb = ln_mix_g[l][None], ln_mix_b[l][None]
        flng, flnb = ln_ffn_g[l][None], ln_ffn_b[l][None]
        ng = gdn_norm_g[l][None]

        g, ba, mq, mk, mv = _in_proj(xp, scale_p, shift_p, w_main[l], w_ba[l], rope_p, tm_prompt, p_idx,
                                     lambda i: i % tps)
        o_gdn, s_p = _gdn_prompt(g, ba, gdn_conv_w[l], alog_pad[l], dtb_pad[l], ng, bsz, seq, tt_gdn)
        o_moba = _moba_prompt(mq, mk, mv, bsz, seq)
        xp = _out_ln(o_gdn, o_moba, xp, gate_p, w_out16[l], lng, lnb, tm_prompt, p_idx, alpha)
        xp, fcp = _ffn(xp, fscale_p, fshift_p, fgate_p, w_up16[l], w_down16[l], ffn_conv_w[l], ffn_conv_b[l][None],
                       flng, flnb, tm_prompt, tc_ffn, p_idx, alpha, tiles_per_seq=tps)
        fcp = fcp[tps - 1::tps]
        gcp = g.reshape(bsz, seq, 4 * WIDTH)[:, seq - (GDN_CONV - 1):, :3 * WIDTH]

        gs, bas, mqs, mks, mvs = _in_proj(xs, scale_s, shift_s, w_main[l], w_ba[l], rope_s, ms, zero_idx, zero_idx)
        gs3 = gs.reshape(dbsz, dseq, 4 * WIDTH)
        o_gdn_s, s_s = _gdn_sample(gs3, bas.reshape(dbsz, dseq, LANES), state_gdn_conv[l], state_gdn[l],
                                   gdn_conv_w[l], alog_pad[l], dtb_pad[l], ng, dbsz, dseq)
        mq3, mk3, mv3 = (a.reshape(dbsz, dseq, WIDTH) for a in (mqs, mks, mvs))
        mq8, mk8, mv8 = (jnp.pad(a, ((0, 0), (0, QPAD - dseq), (0, 0))) for a in (mq3, mk3, mv3))
        kmean, pm, pls, pacc = _moba_partials(pt_flat, mq8, cache_k2, cache_v2, l, dbsz, nb_past, n_pages)
        o_moba_s = _moba_merge(mq8, mk8, mv8, kmean, pm, pls, pacc, dbsz, dseq, nb_past)
        xs = _out_ln(o_gdn_s.reshape(ms, WIDTH), o_moba_s.reshape(ms, WIDTH), xs, gate_s, w_out16[l], lng, lnb,
                     ms, zero_idx, alpha)
        fbuf = state_ffn_conv[l]
        prev1 = jnp.broadcast_to(fbuf[:, 1:2, :], (dbsz, dseq, d_ff)).reshape(ms, d_ff)
        prev2 = jnp.concatenate([fbuf, jnp.zeros((dbsz, dseq - 2, d_ff), F32)], axis=1).reshape(ms, d_ff)
        xs, gfull = _ffn(xs, fscale_s, fshift_s, fgate_s, w_up16[l], w_down16[l], ffn_conv_w[l], ffn_conv_b[l][None],
                         flng, flnb, ms, tc_ffn, zero_idx, alpha, prev_rows=(prev1, prev2), dec_seq=dseq)
        gcs = jnp.concatenate([state_gdn_conv[l], gs3[:, :, :3 * WIDTH]], axis=1)[:, dseq:, :]
        fcs = jnp.concatenate([fbuf, gfull.reshape(dbsz, dseq, d_ff)], axis=1)[:, dseq:, :]

        for lst, val in zip(outs, (mk.reshape(bsz, seq, nh, HEAD_DIM), mv.reshape(bsz, seq, nh, HEAD_DIM),
                                   mk3.reshape(dbsz, dseq, nh, HEAD_DIM), mv3.reshape(dbsz, dseq, nh, HEAD_DIM),
                                   s_p, s_s, gcp, gcs, fcp, fcs)):
            lst.append(val)

    return (xp.reshape(bsz, seq, d), xs.reshape(dbsz, dseq, d)) + tuple(jnp.stack(o) for o in outs)


def kernel(x_prompt, x_sample, cache_k, cache_v, state_gdn, state_gdn_conv, state_ffn_conv, page_table, c_prompt, c_sample, w_ada_mix, b_ada_mix, w_in, gdn_conv_w, gdn_a_log, gdn_dt_bias, gdn_norm_g, w_out, ln_mix_g, ln_mix_b, w_ada_ffn, b_ada_ffn, w_up, ffn_conv_w, ffn_conv_b, w_down, ln_ffn_g, ln_ffn_b):
    return _forward(x_prompt, x_sample, cache_k, cache_v, state_gdn, state_gdn_conv, state_ffn_conv, page_table,
                    c_prompt, c_sample, w_ada_mix, b_ada_mix, w_in, gdn_conv_w, gdn_a_log, gdn_dt_bias, gdn_norm_g,
                    w_out, ln_mix_g, ln_mix_b, w_ada_ffn, b_ada_ffn, w_up, ffn_conv_w, ffn_conv_b, w_down,
                    ln_ffn_g, ln_ffn_b, past_len=8192, tm_prompt=512, tc_ffn=512, tt_gdn=256)
```

```python
import functools
import math

import jax
import jax.numpy as jnp
from jax import lax
from jax.experimental import pallas as pl
from jax.experimental.pallas import tpu as pltpu

F32 = jnp.float32
MXU_DTYPE = jnp.bfloat16

HEAD_DIM = 128
LANES = 128
SUBLANES = 8
N_HEADS = 8
WIDTH = N_HEADS * HEAD_DIM
GDN_CONV = 4
FFN_CONV = 3
MOBA_BLOCK = 256
MOBA_TOPK = 3
PAGE_SIZE = 128
ROPE_THETA = 500000.0
ROPE_DIM = HEAD_DIM // 4
LN_EPS = 1e-5
NORM_EPS = 1e-6
GDN_CHUNK = 128
INV_BLOCK = 16
VMEM_LIMIT = 56 * 1024 * 1024
ROW_SUBTILE = 256
NEG_INF = float("-inf")


def _cparams(*sem):
    return pltpu.CompilerParams(dimension_semantics=sem, vmem_limit_bytes=VMEM_LIMIT)


def _mm(a, b):
    return jnp.dot(a.astype(MXU_DTYPE), b.astype(MXU_DTYPE), preferred_element_type=F32)


def _mm_nt(a, b):
    return lax.dot_general(a.astype(MXU_DTYPE), b.astype(MXU_DTYPE), (((1,), (1,)), ((), ())),
                           preferred_element_type=F32)


def _mm_tn(a, b):
    return lax.dot_general(a.astype(MXU_DTYPE), b.astype(MXU_DTYPE), (((0,), (0,)), ((), ())),
                           preferred_element_type=F32)


def _silu(x):
    return x * jax.nn.sigmoid(x)


def _layer_norm_rows(r, g, b):
    mu = jnp.mean(r, axis=-1, keepdims=True)
    d = r - mu
    var = jnp.mean(d * d, axis=-1, keepdims=True)
    return d * lax.rsqrt(var + LN_EPS) * g + b


def _adaln_kernel(c_ref, w_ref, b_ref, o_ref):
    o_ref[...] = _mm(c_ref[...], w_ref[...]) + b_ref[...]


def _adaln(c, w, b):
    n_layers, d, n = w.shape
    rows = c.shape[0]
    tn = 1024
    return pl.pallas_call(
        _adaln_kernel,
        out_shape=jax.ShapeDtypeStruct((n_layers, rows, n), F32),
        grid=(n_layers, n // tn),
        in_specs=[pl.BlockSpec((rows, d), lambda l, j: (0, 0)),
                  pl.BlockSpec((None, d, tn), lambda l, j: (l, 0, j)),
                  pl.BlockSpec((None, 1, tn), lambda l, j: (l, 0, j))],
        out_specs=pl.BlockSpec((None, rows, tn), lambda l, j: (l, 0, j)),
        compiler_params=_cparams("arbitrary", "arbitrary"),
        name="adaln",
    )(c, w, b.reshape(n_layers, 1, n))


def _rope_store(y, c, s1, s2, o_ref):
    half = ROPE_DIM // 2
    for h in range(y.shape[1] // HEAD_DIM):
        yh = y[:, h * HEAD_DIM:(h + 1) * HEAD_DIM]
        o_ref[:, h * HEAD_DIM:(h + 1) * HEAD_DIM] = (
            yh * c + pltpu.roll(yh, HEAD_DIM - half, 1) * s1 + pltpu.roll(yh, half, 1) * s2)


def _in_proj_kernel(x_ref, sc_ref, sh_ref, w_ref, wba_ref, rc_ref, rs1_ref, rs2_ref,
                    g_ref, ba_ref, mq_ref, mk_ref, mv_ref, h_ref, *, n_g, n_m):
    j = pl.program_id(1)

    @pl.when(j == 0)
    def _():
        h_ref[...] = (x_ref[...] * (1.0 + sc_ref[...]) + sh_ref[...]).astype(h_ref.dtype)
        ba_ref[...] = jnp.dot(h_ref[...], wba_ref[...], preferred_element_type=F32)

    def proj():
        return jnp.dot(h_ref[...], w_ref[...], preferred_element_type=F32)

    @pl.when(j < n_g)
    def _():
        g_ref[...] = proj()

    @pl.when((j >= n_g) & (j < n_g + n_m))
    def _():
        _rope_store(proj(), rc_ref[...], rs1_ref[...], rs2_ref[...], mq_ref)

    @pl.when((j >= n_g + n_m) & (j < n_g + 2 * n_m))
    def _():
        _rope_store(proj(), rc_ref[...], rs1_ref[...], rs2_ref[...], mk_ref)

    @pl.when(j >= n_g + 2 * n_m)
    def _():
        mv_ref[...] = proj()


def _in_proj(x, scale, shift, w_all, w_ba, rope, tm, tn, mod_index, rope_index):
    m, d = x.shape
    r = scale.shape[1]
    n_g, n_m = 4 * WIDTH // tn, WIDTH // tn
    mod_spec = pl.BlockSpec((None, r, d), lambda i, j: (mod_index(i), 0, 0))
    rope_spec = pl.BlockSpec((tm, LANES), lambda i, j: (rope_index(i), 0))

    def m_spec(first):
        return pl.BlockSpec((tm, tn), lambda i, j: (i, jnp.clip(j - first, 0, n_m - 1)))

    return pl.pallas_call(
        functools.partial(_in_proj_kernel, n_g=n_g, n_m=n_m),
        out_shape=(jax.ShapeDtypeStruct((m, 4 * WIDTH), F32),
                   jax.ShapeDtypeStruct((m, LANES), F32),
                   jax.ShapeDtypeStruct((m, WIDTH), F32),
                   jax.ShapeDtypeStruct((m, WIDTH), F32),
                   jax.ShapeDtypeStruct((m, WIDTH), F32)),
        grid=(m // tm, n_g + 3 * n_m),
        in_specs=[pl.BlockSpec((tm, d), lambda i, j: (i, 0)), mod_spec, mod_spec,
                  pl.BlockSpec((d, tn), lambda i, j: (0, j)),
                  pl.BlockSpec((d, LANES), lambda i, j: (0, 0)),
                  rope_spec, rope_spec, rope_spec],
        out_specs=(pl.BlockSpec((tm, tn), lambda i, j: (i, jnp.minimum(j, n_g - 1))),
                   pl.BlockSpec((tm, LANES), lambda i, j: (i, 0)),
                   m_spec(n_g), m_spec(n_g + n_m), m_spec(n_g + 2 * n_m)),
        scratch_shapes=[pltpu.VMEM((tm, d), MXU_DTYPE)],
        compiler_params=_cparams("arbitrary", "arbitrary"),
        name="in_proj",
    )(x, scale, shift, w_all, w_ba, *rope)


def _unit_lower_inverse(a_low, eye, same_blk):
    items = range(len(a_low))
    d = [jnp.where(same_blk, -a, 0.0) for a in a_low]
    o = [-a_low[i] - d[i] for i in items]
    x = [eye + d[i] for i in items]
    p = d
    for _ in range(int(math.log2(INV_BLOCK)) - 1):
        p = [_mm(p[i], p[i]) for i in items]
        x = [x[i] + _mm(x[i], p[i]) for i in items]
    y = [_mm(x[i], o[i]) for i in items]
    z = [eye + y[i] for i in items]
    p = y
    for _ in range(int(math.log2(GDN_CHUNK // INV_BLOCK)) - 1):
        p = [_mm(p[i], p[i]) for i in items]
        z = [z[i] + _mm(z[i], p[i]) for i in items]
    return z, x


def _gdn_wy(q, k, v, beta_c, eg_c, gc_col, gc_row, masks):
    eye, same_blk, tri_incl, tri_strict = masks
    items = range(len(q))
    c = q[0].shape[0]
    kb = [k[i] * beta_c[i] for i in items]
    p = [_mm_nt(jnp.concatenate([q[i], kb[i]], axis=0), k[i]) for i in items]
    decay = [jnp.exp(jnp.where(tri_incl, gc_col[i] - gc_row[i], NEG_INF)) for i in items]
    qk = [p[i][:c] * decay[i] for i in items]
    a_low = [jnp.where(tri_strict, p[i][c:] * decay[i], 0.0) for i in items]
    z, x = _unit_lower_inverse(a_low, eye, same_blk)
    rhs = [jnp.concatenate([v[i] * beta_c[i], kb[i] * eg_c[i]], axis=1) for i in items]
    t = [_mm(x[i], rhs[i]) for i in items]
    sol = [_mm(z[i], t[i]) for i in items]
    return [s[:, :HEAD_DIM] for s in sol], [s[:, HEAD_DIM:] for s in sol], qk


def _gdn_state_step(q, k, u, w, qk, eg_c, ek_c, gl, s):
    items = range(len(q))
    v_new = [u[i] - _mm(w[i], s[i]) for i in items]
    o = [_mm(q[i] * eg_c[i], s[i]) + _mm(qk[i], v_new[i]) for i in items]
    s_new = [s[i] * gl[i] + _mm_tn(k[i] * ek_c[i], v_new[i]) for i in items]
    return o, s_new


def _gdn_kernel(x_ref, ba_ref, cw_ref, alog_ref, dtb_ref, ng_ref, *rest, tt, n_valid, from_state):
    if from_state:
        cs_ref, s0_ref, o_ref, sout_ref, xbuf, zbuf, babuf, s_scr = rest
    else:
        o_ref, sout_ref, xbuf, s_scr = rest
    t = pl.program_id(1)
    nk = GDN_CONV - 1
    c = GDN_CHUNK

    if from_state:
        xbuf[...] = jnp.zeros(xbuf.shape, F32)
        xbuf[SUBLANES - nk:SUBLANES, :] = cs_ref[...]
        xbuf[SUBLANES:SUBLANES + n_valid, :] = x_ref[:, 0:3 * WIDTH]
        zbuf[...] = jnp.zeros(zbuf.shape, F32)
        zbuf[0:n_valid, :] = x_ref[:, 3 * WIDTH:4 * WIDTH]
        babuf[...] = jnp.zeros(babuf.shape, F32)
        babuf[0:n_valid, :] = ba_ref[...]
        s_scr[...] = s0_ref[...]
        ba = babuf[...]
    else:
        @pl.when(t == 0)
        def _():
            xbuf[0:SUBLANES, :] = jnp.zeros((SUBLANES, 3 * WIDTH), F32)
            s_scr[...] = jnp.zeros(s_scr.shape, F32)

        @pl.when(t > 0)
        def _():
            xbuf[0:SUBLANES, :] = xbuf[tt:tt + SUBLANES, :]

        xbuf[SUBLANES:SUBLANES + tt, :] = x_ref[:, 0:3 * WIDTH]
        ba = ba_ref[...]

    beta_all = jax.nn.sigmoid(ba)
    a_in = ba + dtb_ref[...]
    softplus = jnp.maximum(a_in, 0.0) + jnp.log1p(jnp.exp(-jnp.abs(a_in)))
    g_all = -jnp.exp(alog_ref[...]) * softplus
    if n_valid < tt:
        live = lax.broadcasted_iota(jnp.int32, (tt, LANES), 0) < n_valid
        beta_all = jnp.where(live, beta_all, 0.0)
        g_all = jnp.where(live, g_all, 0.0)

    row = lax.broadcasted_iota(jnp.int32, (c, c), 0)
    col = lax.broadcasted_iota(jnp.int32, (c, c), 1)
    tri_incl = row >= col
    tri_strict = row > col
    masks = (jnp.where(row == col, 1.0, 0.0).astype(F32), (row // INV_BLOCK) == (col // INV_BLOCK),
             tri_incl, tri_strict)
    ltri = jnp.where(tri_incl, 1.0, 0.0).astype(F32)

    def conv_act(col0):
        acc = xbuf[SUBLANES - nk:SUBLANES - nk + tt, col0:col0 + HEAD_DIM] * cw_ref[0:1, col0:col0 + HEAD_DIM]
        for i in range(1, GDN_CONV):
            acc = acc + (xbuf[SUBLANES - nk + i:SUBLANES - nk + i + tt, col0:col0 + HEAD_DIM]
                         * cw_ref[i:i + 1, col0:col0 + HEAD_DIM])
        return _silu(acc)

    def l2n(a):
        return a * lax.rsqrt(jnp.sum(a * a, axis=-1, keepdims=True) + NORM_EPS)

    chunk_decay = []
    for ci in range(tt // c):
        gcum = jnp.dot(ltri, g_all[ci * c:(ci + 1) * c, :], precision=lax.Precision.HIGHEST,
                       preferred_element_type=F32)
        chunk_decay.append((gcum, gcum.T, jnp.exp(gcum), jnp.exp(gcum[c - 1:c, :] - gcum)))

    heads = range(N_HEADS)
    n_chunks = tt // c
    q_t = [l2n(conv_act(h * HEAD_DIM)) * (HEAD_DIM ** -0.5) for h in heads]
    k_t = [l2n(conv_act(WIDTH + h * HEAD_DIM)) for h in heads]
    v_t = [conv_act(2 * WIDTH + h * HEAD_DIM) for h in heads]

    pairs = [(ci, h) for ci in range(n_chunks) for h in heads]
    rows = lambda a, ci: a[ci * c:(ci + 1) * c]
    lane = lambda a, h: a[:, N_HEADS + h:N_HEADS + h + 1]
    qs = [rows(q_t[h], ci) for ci, h in pairs]
    ks = [rows(k_t[h], ci) for ci, h in pairs]
    egs = [lane(chunk_decay[ci][2], h) for ci, h in pairs]
    u, w, qk = _gdn_wy(
        qs, ks, [rows(v_t[h], ci) for ci, h in pairs],
        [rows(beta_all, ci)[:, h:h + 1] for ci, h in pairs], egs,
        [lane(chunk_decay[ci][0], h) for ci, h in pairs],
        [chunk_decay[ci][1][N_HEADS + h:N_HEADS + h + 1, :] for ci, h in pairs], masks)

    s = [s_scr[h] for h in heads]
    for ci in range(n_chunks):
        sl = slice(ci * N_HEADS, (ci + 1) * N_HEADS)
        eks = [lane(chunk_decay[ci][3], h) for h in heads]
        gls = [lane(chunk_decay[ci][2], h)[c - 1:c] for h in heads]
        o, s = _gdn_state_step(qs[sl], ks[sl], u[sl], w[sl], qk[sl], egs[sl], eks, gls, s)
        r0 = ci * c
        for h in heads:
            hs = slice(h * HEAD_DIM, (h + 1) * HEAD_DIM)
            if from_state:
                zh = zbuf[r0:r0 + c, hs]
            else:
                zh = x_ref[r0:r0 + c, 3 * WIDTH + h * HEAD_DIM:3 * WIDTH + (h + 1) * HEAD_DIM]
            on = o[h] * lax.rsqrt(jnp.mean(o[h] * o[h], axis=-1, keepdims=True) + NORM_EPS) * ng_ref[...] * _silu(zh)
            if from_state:
                o_ref[:, hs] = on[0:n_valid].astype(o_ref.dtype)
            else:
                o_ref[r0:r0 + c, hs] = on.astype(o_ref.dtype)
    for h in heads:
        s_scr[h] = s[h]
        sout_ref[h] = s[h]


def _gdn_prompt(g, ba, conv_w, alog_pad, dtb_pad, norm_g, batch, seq, tt):
    m = g.shape[0]
    nt = seq // tt
    kern = functools.partial(_gdn_kernel, tt=tt, n_valid=tt, from_state=False)
    small = lambda shape: pl.BlockSpec(shape, lambda b, t: (0, 0))
    return pl.pallas_call(
        kern,
        out_shape=(jax.ShapeDtypeStruct((m, WIDTH), MXU_DTYPE),
                   jax.ShapeDtypeStruct((batch, N_HEADS, HEAD_DIM, HEAD_DIM), F32)),
        grid=(batch, nt),
        in_specs=[pl.BlockSpec((tt, 4 * WIDTH), lambda b, t: (b * nt + t, 0)),
                  pl.BlockSpec((tt, LANES), lambda b, t: (b * nt + t, 0)),
                  small((GDN_CONV, 3 * WIDTH)), small((1, LANES)), small((1, LANES)), small((1, HEAD_DIM))],
        out_specs=(pl.BlockSpec((tt, WIDTH), lambda b, t: (b * nt + t, 0)),
                   pl.BlockSpec((None, N_HEADS, HEAD_DIM, HEAD_DIM), lambda b, t: (b, 0, 0, 0))),
        scratch_shapes=[pltpu.VMEM((tt + SUBLANES, 3 * WIDTH), F32),
                        pltpu.VMEM((N_HEADS, HEAD_DIM, HEAD_DIM), F32)],
        compiler_params=_cparams("arbitrary", "arbitrary"),
        name="gdn_prompt",
    )(g, ba, conv_w, alog_pad, dtb_pad, norm_g)


def _gdn_sample(g, ba, conv_state, s0, conv_w, alog_pad, dtb_pad, norm_g, batch, dec_seq):
    tt = GDN_CHUNK
    kern = functools.partial(_gdn_kernel, tt=tt, n_valid=dec_seq, from_state=True)
    small = lambda shape: pl.BlockSpec(shape, lambda b, t: (0, 0))
    return pl.pallas_call(
        kern,
        out_shape=(jax.ShapeDtypeStruct((batch, dec_seq, WIDTH), F32),
                   jax.ShapeDtypeStruct((batch, N_HEADS, HEAD_DIM, HEAD_DIM), F32)),
        grid=(batch, 1),
        in_specs=[pl.BlockSpec((None, dec_seq, 4 * WIDTH), lambda b, t: (b, 0, 0)),
                  pl.BlockSpec((None, dec_seq, LANES), lambda b, t: (b, 0, 0)),
                  small((GDN_CONV, 3 * WIDTH)), small((1, LANES)), small((1, LANES)), small((1, HEAD_DIM)),
                  pl.BlockSpec((None, GDN_CONV - 1, 3 * WIDTH), lambda b, t: (b, 0, 0)),
                  pl.BlockSpec((None, N_HEADS, HEAD_DIM, HEAD_DIM), lambda b, t: (b, 0, 0, 0))],
        out_specs=(pl.BlockSpec((None, dec_seq, WIDTH), lambda b, t: (b, 0, 0)),
                   pl.BlockSpec((None, N_HEADS, HEAD_DIM, HEAD_DIM), lambda b, t: (b, 0, 0, 0))),
        scratch_shapes=[pltpu.VMEM((tt + SUBLANES, 3 * WIDTH), F32),
                        pltpu.VMEM((tt, WIDTH), F32),
                        pltpu.VMEM((tt, LANES), F32),
                        pltpu.VMEM((N_HEADS, HEAD_DIM, HEAD_DIM), F32)],
        compiler_params=_cparams("arbitrary", "arbitrary"),
        name="gdn_sample",
    )(g, ba, conv_w, alog_pad, dtb_pad, norm_g, conv_state, s0)


def _topk_select(gate, n_valid, idx, axis):
    n = gate.shape[axis]
    cnt = jnp.zeros(gate.shape, F32)
    for j in range(n):
        cand = lax.slice_in_dim(gate, j, j + 1, axis=axis)
        beats = jnp.where(cand > gate, 1.0, jnp.where((cand == gate) & (j < idx), 1.0, 0.0))
        cnt = cnt + jnp.where(j < n_valid, beats, 0.0)
    return (cnt < MOBA_TOPK) & (idx < n_valid)


def _moba_prompt_kernel(q_ref, k_ref, v_ref, o_ref, k16, vt16, kmean, *, nb):
    i = pl.program_id(2)
    blk = MOBA_BLOCK
    scale = HEAD_DIM ** -0.5

    @pl.when(i == 0)
    def _():
        for j in range(nb):
            kj = k_ref[j * blk:(j + 1) * blk, :]
            k16[j] = kj.astype(MXU_DTYPE)
            kmean[j:j + 1, :] = jnp.mean(kj, axis=0, keepdims=True)
            vt16[j] = v_ref[j * blk:(j + 1) * blk, :].T.astype(MXU_DTYPE)

    kpos = lax.broadcasted_iota(jnp.int32, (blk, blk), 0)
    qpos = lax.broadcasted_iota(jnp.int32, (blk, blk), 1)

    def attend(cur):
        qt = q_ref[...].T
        qt16 = qt.astype(MXU_DTYPE)
        s = []
        if cur > MOBA_TOPK:
            gate = jnp.dot(kmean[...], qt, precision=lax.Precision.HIGHEST, preferred_element_type=F32)
            blk_idx = lax.broadcasted_iota(jnp.int32, gate.shape, 0)
            sel = jnp.where(_topk_select(gate, cur, blk_idx, 0), 1.0, 0.0)
        for j in range(cur):
            sj = jnp.dot(k16[j], qt16, preferred_element_type=F32) * scale
            if cur > MOBA_TOPK:
                sj = jnp.where(sel[j:j + 1, :] > 0.5, sj, NEG_INF)
            s.append(sj)
        s.append(jnp.where(kpos <= qpos, jnp.dot(k16[cur], qt16, preferred_element_type=F32) * scale, NEG_INF))
        m = jnp.max(s[0], axis=0, keepdims=True)
        for sj in s[1:]:
            m = jnp.maximum(m, jnp.max(sj, axis=0, keepdims=True))
        l = jnp.zeros((1, blk), F32)
        acc = jnp.zeros((HEAD_DIM, blk), F32)
        for j, sj in enumerate(s):
            p = jnp.exp(sj - m)
            l = l + jnp.sum(p, axis=0, keepdims=True)
            acc = acc + jnp.dot(vt16[j], p.astype(MXU_DTYPE), preferred_element_type=F32)
        o_ref[...] = (acc / l).T.astype(o_ref.dtype)

    for cur in range(nb):
        pl.when(i == cur)(functools.partial(attend, cur))


def _moba_prompt(mq, mk, mv, batch, seq):
    m = mq.shape[0]
    nb = seq // MOBA_BLOCK
    kern = functools.partial(_moba_prompt_kernel, nb=nb)
    kv_spec = pl.BlockSpec((seq, HEAD_DIM), lambda b, h, i: (b, h))
    return pl.pallas_call(
        kern,
        out_shape=jax.ShapeDtypeStruct((m, WIDTH), MXU_DTYPE),
        grid=(batch, N_HEADS, nb),
        in_specs=[pl.BlockSpec((MOBA_BLOCK, HEAD_DIM), lambda b, h, i: (b * nb + i, h)), kv_spec, kv_spec],
        out_specs=pl.BlockSpec((MOBA_BLOCK, HEAD_DIM), lambda b, h, i: (b * nb + i, h)),
        scratch_shapes=[pltpu.VMEM((nb, MOBA_BLOCK, HEAD_DIM), MXU_DTYPE),
                        pltpu.VMEM((nb, HEAD_DIM, MOBA_BLOCK), MXU_DTYPE),
                        pltpu.VMEM((nb, HEAD_DIM), F32)],
        compiler_params=_cparams("arbitrary", "arbitrary", "arbitrary"),
        name="moba_prompt",
    )(mq, mk, mv)


QPAD = SUBLANES


def _moba_partial_kernel(pt_ref, q_ref, k0_ref, k1_ref, v0_ref, v1_ref, kmean_ref, pm_ref, pl_ref, pacc_ref):
    n = pl.program_id(1)
    scale = HEAD_DIM ** -0.5
    rows = PAGE_SIZE * N_HEADS
    pages = [(k0_ref[...], v0_ref), (k1_ref[...], v1_ref)]
    kmean_ref[n] = (jnp.sum(pages[0][0], axis=0) + jnp.sum(pages[1][0], axis=0)) * (1.0 / MOBA_BLOCK)
    q16 = q_ref[...].astype(MXU_DTYPE)
    q_head = lax.broadcasted_iota(jnp.int32, (N_HEADS * QPAD, rows), 0) // QPAD
    k_head = lax.broadcasted_iota(jnp.int32, (N_HEADS * QPAD, rows), 1) % N_HEADS
    same_head = q_head == k_head
    s = [jnp.where(same_head, _mm_nt(q16, k.reshape(rows, HEAD_DIM)) * scale, NEG_INF) for k, _ in pages]
    mx = jnp.maximum(jnp.max(s[0], axis=-1, keepdims=True), jnp.max(s[1], axis=-1, keepdims=True))
    p = [jnp.exp(sp - mx) for sp in s]
    lsum = jnp.sum(p[0], axis=-1, keepdims=True) + jnp.sum(p[1], axis=-1, keepdims=True)
    pm_ref[...] = jnp.broadcast_to(mx, pm_ref.shape)
    pl_ref[...] = jnp.broadcast_to(lsum, pl_ref.shape)
    pacc_ref[...] = (_mm(p[0], pages[0][1][...].reshape(rows, HEAD_DIM))
                     + _mm(p[1], pages[1][1][...].reshape(rows, HEAD_DIM)))


def _moba_partials(page_table_flat, q, cache_k, cache_v, layer, batch, nb, n_pages):
    ppb = MOBA_BLOCK // PAGE_SIZE
    assert ppb == 2
    hq = N_HEADS * QPAD

    def page_spec(which):
        return pl.BlockSpec((None, None, PAGE_SIZE, N_HEADS, HEAD_DIM),
                            lambda b, n, pt: (layer, pt[b * n_pages + ppb * n + which], 0, 0, 0))

    part_shape = jax.ShapeDtypeStruct((batch, nb, hq, LANES), F32)
    part_spec = pl.BlockSpec((None, None, hq, LANES), lambda b, n, pt: (b, n, 0, 0))
    return pl.pallas_call(
        _moba_partial_kernel,
        out_shape=(jax.ShapeDtypeStruct((batch, nb, N_HEADS, HEAD_DIM), F32), part_shape, part_shape, part_shape),
        grid_spec=pltpu.PrefetchScalarGridSpec(
            num_scalar_prefetch=1,
            grid=(batch, nb),
            in_specs=[pl.BlockSpec((None, hq, HEAD_DIM), lambda b, n, pt: (b, 0, 0)),
                      page_spec(0), page_spec(1), page_spec(0), page_spec(1)],
            out_specs=(pl.BlockSpec((None, nb, N_HEADS, HEAD_DIM), lambda b, n, pt: (b, 0, 0, 0)),
                       part_spec, part_spec, part_spec)),
        compiler_params=_cparams("arbitrary", "arbitrary"),
        name="moba_sample_partials",
    )(page_table_flat, q, cache_k, cache_k, cache_v, cache_v)


def _moba_merge_kernel(q_ref, k_ref, v_ref, kmean_ref, pm_ref, pl_ref, pacc_ref, o_ref, *, dec_seq, nb):
    scale = HEAD_DIM ** -0.5
    q8, k8, v8 = q_ref[...], k_ref[...], v_ref[...]
    qi = lax.broadcasted_iota(jnp.int32, (QPAD, QPAD), 0)
    kj = lax.broadcasted_iota(jnp.int32, (QPAD, QPAD), 1)
    own_ok = (kj <= qi) & (kj < dec_seq)
    blk_idx = lax.broadcasted_iota(jnp.int32, (QPAD, nb), 1)
    for h in range(N_HEADS):
        hs = slice(h * HEAD_DIM, (h + 1) * HEAD_DIM)
        hq = slice(h * QPAD, (h + 1) * QPAD)
        gate = lax.dot_general(q8[:, hs], kmean_ref[:, h, :], (((1,), (1,)), ((), ())),
                               precision=lax.Precision.HIGHEST, preferred_element_type=F32)
        sel = jnp.where(_topk_select(gate, nb, blk_idx, 1), 1.0, 0.0)
        s_own = jnp.where(own_ok, _mm_nt(q8[:, hs], k8[:, hs]) * scale, NEG_INF)
        m_tot = jnp.broadcast_to(jnp.max(s_own, axis=-1, keepdims=True), (QPAD, LANES))
        for n in range(nb):
            m_tot = jnp.maximum(m_tot, jnp.where(sel[:, n:n + 1] > 0.5, pm_ref[n, hq, :], NEG_INF))
        p_own = jnp.exp(s_own - m_tot[:, 0:1])
        l_tot = jnp.broadcast_to(jnp.sum(p_own, axis=-1, keepdims=True), (QPAD, LANES))
        acc = jnp.zeros((QPAD, HEAD_DIM), F32)
        for j in range(dec_seq):
            acc = acc + p_own[:, j:j + 1] * v8[j:j + 1, hs]
        for n in range(nb):
            w = jnp.where(sel[:, n:n + 1] > 0.5, jnp.exp(pm_ref[n, hq, :] - m_tot), 0.0)
            l_tot = l_tot + w * pl_ref[n, hq, :]
            acc = acc + w * pacc_ref[n, hq, :]
        o_ref[:, hs] = (acc / l_tot)[0:dec_seq].astype(o_ref.dtype)


def _moba_merge(q, k, v, kmean, pm, plsum, pacc, batch, dec_seq, nb):
    kern = functools.partial(_moba_merge_kernel, dec_seq=dec_seq, nb=nb)
    row_spec = pl.BlockSpec((None, QPAD, WIDTH), lambda b: (b, 0, 0))
    part_spec = pl.BlockSpec((None, nb, N_HEADS * QPAD, LANES), lambda b: (b, 0, 0, 0))
    return pl.pallas_call(
        kern,
        out_shape=jax.ShapeDtypeStruct((batch, dec_seq, WIDTH), F32),
        grid=(batch,),
        in_specs=[row_spec, row_spec, row_spec,
                  pl.BlockSpec((None, nb, N_HEADS, HEAD_DIM), lambda b: (b, 0, 0, 0)),
                  part_spec, part_spec, part_spec],
        out_specs=pl.BlockSpec((None, dec_seq, WIDTH), lambda b: (b, 0, 0)),
        compiler_params=_cparams("arbitrary"),
        name="moba_sample_merge",
    )(q, k, v, kmean, pm, plsum, pacc)


def _out_ln_kernel(og_ref, om_ref, x_ref, gate_ref, w1_ref, w2_ref, lng_ref, lnb_ref, o_ref, *, alpha):
    tm = x_ref.shape[0]
    rs = min(tm, ROW_SUBTILE)
    subs = [slice(s * rs, (s + 1) * rs) for s in range(tm // rs)]
    y = [_mm(og_ref[sl, :], w1_ref[...]) + _mm(om_ref[sl, :], w2_ref[...]) for sl in subs]
    for sl, ys in zip(subs, y):
        gate = gate_ref[...] if gate_ref.shape[0] == 1 else gate_ref[sl, :]
        r = alpha * x_ref[sl, :] + (1.0 + gate) * ys
        o_ref[sl, :] = _layer_norm_rows(r, lng_ref[...], lnb_ref[...])


def _out_ln(o_gdn, o_moba, x, gate, w_out, ln_g, ln_b, tm, mod_index, alpha):
    m, d = x.shape
    r = gate.shape[1]
    vec = pl.BlockSpec((1, d), lambda i: (0, 0))
    return pl.pallas_call(
        functools.partial(_out_ln_kernel, alpha=alpha),
        out_shape=jax.ShapeDtypeStruct((m, d), F32),
        grid=(m // tm,),
        in_specs=[pl.BlockSpec((tm, WIDTH), lambda i: (i, 0)), pl.BlockSpec((tm, WIDTH), lambda i: (i, 0)),
                  pl.BlockSpec((tm, d), lambda i: (i, 0)),
                  pl.BlockSpec((None, r, d), lambda i: (mod_index(i), 0, 0)),
                  pl.BlockSpec((WIDTH, d), lambda i: (0, 0)), pl.BlockSpec((WIDTH, d), lambda i: (1, 0)),
                  vec, vec],
        out_specs=pl.BlockSpec((tm, d), lambda i: (i, 0)),
        compiler_params=_cparams("arbitrary"),
        name="out_proj_ln",
    )(o_gdn, o_moba, x, gate, w_out, w_out, ln_g, ln_b)


def _ffn_kernel(x_ref, sc_ref, sh_ref, gate_ref, wg_ref, wu_ref, wd_ref, cw_ref, cb_ref, lng_ref, lnb_ref, *rest,
                tm, n_sub, n_chunks, tiles_per_seq, dec_seq, alpha):
    sample = dec_seq is not None
    if sample:
        p1_ref, p2_ref, o_ref, st_ref, h_ref, acc_ref, gbuf = rest
    else:
        o_ref, st_ref, h_ref, acc_ref, gbuf, carry = rest
    i = pl.program_id(0)
    c = pl.program_id(1)

    @pl.when(c == 0)
    def _():
        h_ref[...] = (x_ref[...] * (1.0 + sc_ref[...]) + sh_ref[...]).astype(h_ref.dtype)
        acc_ref[...] = jnp.zeros(acc_ref.shape, F32)

    tc = wg_ref.shape[1]
    rs = tm // n_sub
    subs = [slice(s * rs, (s + 1) * rs) for s in range(n_sub)]
    if sample:
        gbuf[0:SUBLANES, :] = jnp.zeros((SUBLANES, tc), F32)
    else:
        first = (i % tiles_per_seq) == 0

        @pl.when(first)
        def _():
            gbuf[0:SUBLANES, :] = jnp.zeros((SUBLANES, tc), F32)

        @pl.when(jnp.logical_not(first))
        def _():
            gbuf[0:SUBLANES, :] = carry[c]

    g, u = [], []
    for s, sl in enumerate(subs):
        g.append(jnp.dot(h_ref[sl, :], wg_ref[...], preferred_element_type=F32))
        gbuf[SUBLANES + s * rs:SUBLANES + (s + 1) * rs, :] = g[s]
        u.append(jnp.dot(h_ref[sl, :], wu_ref[...], preferred_element_type=F32))
    if not sample:
        carry[c] = g[-1][rs - SUBLANES:rs, :]
        st_ref[...] = g[-1][rs - (FFN_CONV - 1):rs, :]
    for s, sl in enumerate(subs):
        prev1 = gbuf[SUBLANES - 1 + s * rs:SUBLANES - 1 + (s + 1) * rs, :]
        prev2 = gbuf[SUBLANES - 2 + s * rs:SUBLANES - 2 + (s + 1) * rs, :]
        if sample:
            pos = lax.broadcasted_iota(jnp.int32, (rs, tc), 0) % dec_seq
            prev1 = jnp.where(pos >= 1, prev1, p1_ref[sl, :])
            prev2 = jnp.where(pos >= 2, prev2, p2_ref[sl, :])
            st_ref[sl, :] = g[s]
        conv = prev2 * cw_ref[0:1, :] + prev1 * cw_ref[1:2, :] + g[s] * cw_ref[2:3, :]
        act = _silu(conv + cb_ref[...]) * u[s]
        acc_ref[sl, :] += jnp.dot(act.astype(MXU_DTYPE), wd_ref[...], preferred_element_type=F32)

    @pl.when(c == n_chunks - 1)
    def _():
        r = alpha * x_ref[...] + (1.0 + gate_ref[...]) * acc_ref[...]
        o_ref[...] = _layer_norm_rows(r, lng_ref[...], lnb_ref[...])


def _ffn(x, scale, shift, gate, w_up, w_down, conv_w, conv_b, ln_g, ln_b, tm, tc, mod_index, alpha,
         tiles_per_seq=None, batch=None, prev_rows=None, dec_seq=None):
    m, d = x.shape
    d_ff = w_down.shape[0]
    n_chunks = d_ff // tc
    r = scale.shape[1]
    sample = prev_rows is not None
    mod_spec = pl.BlockSpec((None, r, d), lambda i, c: (mod_index(i), 0, 0))
    vec = pl.BlockSpec((1, d), lambda i, c: (0, 0))
    in_specs = [pl.BlockSpec((tm, d), lambda i, c: (i, 0)), mod_spec, mod_spec, mod_spec,
                pl.BlockSpec((d, tc), lambda i, c: (0, c)),
                pl.BlockSpec((d, tc), lambda i, c: (0, n_chunks + c)),
                pl.BlockSpec((tc, d), lambda i, c: (c, 0)),
                pl.BlockSpec((FFN_CONV, tc), lambda i, c: (0, c)),
                pl.BlockSpec((1, tc), lambda i, c: (0, c)),
                vec, vec]
    args = [x, scale, shift, gate, w_up, w_up, w_down, conv_w, conv_b, ln_g, ln_b]
    scratch = [pltpu.VMEM((tm, d), MXU_DTYPE), pltpu.VMEM((tm, d), F32), pltpu.VMEM((tm + SUBLANES, tc), F32)]
    if sample:
        in_specs += [pl.BlockSpec((tm, tc), lambda i, c: (i, c))] * 2
        args += list(prev_rows)
        st_shape = jax.ShapeDtypeStruct((m, d_ff), F32)
        st_spec = pl.BlockSpec((tm, tc), lambda i, c: (i, c))
    else:
        st_shape = jax.ShapeDtypeStruct((m // tm, FFN_CONV - 1, d_ff), F32)
        st_spec = pl.BlockSpec((None, FFN_CONV - 1, tc), lambda i, c: (i, 0, c))
        scratch.append(pltpu.VMEM((n_chunks, SUBLANES, tc), F32))
    kern = functools.partial(_ffn_kernel, tm=tm, n_sub=max(1, tm // ROW_SUBTILE), n_chunks=n_chunks,
                             tiles_per_seq=tiles_per_seq,
                             dec_seq=dec_seq if sample else None, alpha=alpha)
    return pl.pallas_call(
        kern,
        out_shape=(jax.ShapeDtypeStruct((m, d), F32), st_shape),
        grid=(m // tm, n_chunks),
        in_specs=in_specs,
        out_specs=(pl.BlockSpec((tm, d), lambda i, c: (i, 0)), st_spec),
        scratch_shapes=scratch,
        compiler_params=_cparams("arbitrary", "arbitrary"),
        name="ffn_sample" if sample else "ffn_prompt",
    )(*args)


def _rope_tables(pos):
    half = ROPE_DIM // 2
    inv_freq = ROPE_THETA ** (-jnp.arange(half, dtype=F32) / half)
    ang = pos.astype(F32)[:, None] * inv_freq[None, :]
    cos, sin = jnp.cos(ang), jnp.sin(ang)
    n = pos.shape[0]
    ones = jnp.ones((n, HEAD_DIM - ROPE_DIM), F32)
    zeros = jnp.zeros((n, HEAD_DIM - ROPE_DIM), F32)
    zh = jnp.zeros((n, half), F32)
    return (jnp.concatenate([cos, cos, ones], axis=1),
            jnp.concatenate([-sin, zh, zeros], axis=1),
            jnp.concatenate([zh, sin, zeros], axis=1))


def _forward(x_prompt, x_sample, cache_k, cache_v, state_gdn, state_gdn_conv, state_ffn_conv, page_table,
             c_prompt, c_sample, w_ada_mix, b_ada_mix, w_in, gdn_conv_w, gdn_a_log, gdn_dt_bias, gdn_norm_g,
             w_out, ln_mix_g, ln_mix_b, w_ada_ffn, b_ada_ffn, w_up, ffn_conv_w, ffn_conv_b, w_down,
             ln_ffn_g, ln_ffn_b, *, past_len, tm_prompt, tm_in, tn_in, tc_ffn, tt_gdn):
    bsz, seq, d = x_prompt.shape
    dbsz, dseq, _ = x_sample.shape
    depth = w_in.shape[0]
    d_ff = w_down.shape[1]
    alpha = (2 * depth) ** 0.25
    n_pool = cache_k.shape[1]
    n_pages = page_table.shape[1]
    nb_past = past_len // MOBA_BLOCK
    assert d == 2 * WIDTH and past_len % MOBA_BLOCK == 0 and nb_past >= MOBA_TOPK
    assert n_pages * PAGE_SIZE == past_len and seq % MOBA_BLOCK == 0
    mp, ms = bsz * seq, dbsz * dseq
    tps = seq // tm_prompt

    gw, nh = WIDTH, N_HEADS
    w_main = jnp.concatenate([w_in[:, :, :4 * gw], w_in[:, :, 4 * gw + 2 * nh:]], axis=2).astype(MXU_DTYPE)
    w_ba = jnp.pad(w_in[:, :, 4 * gw:4 * gw + 2 * nh], ((0, 0), (0, 0), (0, LANES - 2 * nh))).astype(MXU_DTYPE)
    w_out16, w_up16, w_down16 = (w.astype(MXU_DTYPE) for w in (w_out, w_up, w_down))
    alog_pad = jnp.pad(gdn_a_log, ((0, 0), (nh, LANES - 2 * nh)))[:, None, :]
    dtb_pad = jnp.pad(gdn_dt_bias, ((0, 0), (nh, LANES - 2 * nh)))[:, None, :]

    c_all = jnp.concatenate([c_prompt, c_sample], axis=0)
    mod_mix = _adaln(c_all, w_ada_mix, b_ada_mix)
    mod_ffn = _adaln(c_all, w_ada_ffn, b_ada_ffn)

    def mods(mod, l):
        out_p, out_s = [], []
        for part in range(3):
            v = mod[l, :, part * d:(part + 1) * d]
            out_p.append(v[:bsz, None, :])
            out_s.append(jnp.repeat(v[bsz:], dseq, axis=0)[None])
        return out_p, out_s

    rope_p = _rope_tables(jnp.arange(seq))
    rope_s = _rope_tables(jnp.tile(past_len + jnp.arange(dseq), dbsz))
    pt_flat = page_table.reshape(-1)

    xp = x_prompt.reshape(mp, d)
    xs = x_sample.reshape(ms, d)
    p_idx = lambda i: i // tps
    zero_idx = lambda i: 0
    outs = [[] for _ in range(10)]
    for l in range(depth):
        (shift_p, scale_p, gate_p), (shift_s, scale_s, gate_s) = mods(mod_mix, l)
        (fshift_p, fscale_p, fgate_p), (fshift_s, fscale_s, fgate_s) = mods(mod_ffn, l)
        lng, lnb = ln_mix_g[l][None], ln_mix_b[l][None]
        flng, flnb = ln_ffn_g[l][None], ln_ffn_b[l][None]
        ng = gdn_norm_g[l][None]

        tps_in = seq // tm_in
        g, ba, mq, mk, mv = _in_proj(xp, scale_p, shift_p, w_main[l], w_ba[l], rope_p, tm_in, tn_in,
                                     lambda i: i // tps_in, lambda i: i % tps_in)
        o_gdn, s_p = _gdn_prompt(g, ba, gdn_conv_w[l], alog_pad[l], dtb_pad[l], ng, bsz, seq, tt_gdn)
        o_moba = _moba_prompt(mq, mk, mv, bsz, seq)
        xp = _out_ln(o_gdn, o_moba, xp, gate_p, w_out16[l], lng, lnb, tm_prompt, p_idx, alpha)
        xp, fcp = _ffn(xp, fscale_p, fshift_p, fgate_p, w_up16[l], w_down16[l], ffn_conv_w[l], ffn_conv_b[l][None],
                       flng, flnb, tm_prompt, tc_ffn, p_idx, alpha, tiles_per_seq=tps)
        fcp = fcp[tps - 1::tps]
        gcp = g.reshape(bsz, seq, 4 * WIDTH)[:, seq - (GDN_CONV - 1):, :3 * WIDTH]

        gs, bas, mqs, mks, mvs = _in_proj(xs, scale_s, shift_s, w_main[l], w_ba[l], rope_s, ms, WIDTH,
                                          zero_idx, zero_idx)
        gs3 = gs.reshape(dbsz, dseq, 4 * WIDTH)
        o_gdn_s, s_s = _gdn_sample(gs3, bas.reshape(dbsz, dseq, LANES), state_gdn_conv[l], state_gdn[l],
                                   gdn_conv_w[l], alog_pad[l], dtb_pad[l], ng, dbsz, dseq)
        mq3, mk3, mv3 = (a.reshape(dbsz, dseq, WIDTH) for a in (mqs, mks, mvs))
        mq8, mk8, mv8 = (jnp.pad(a, ((0, 0), (0, QPAD - dseq), (0, 0))) for a in (mq3, mk3, mv3))
        mq_hm = mq8.reshape(dbsz, QPAD, nh, HEAD_DIM).transpose(0, 2, 1, 3).reshape(dbsz, nh * QPAD, HEAD_DIM)
        kmean, pm, pls, pacc = _moba_partials(pt_flat, mq_hm, cache_k, cache_v, l, dbsz, nb_past, n_pages)
        o_moba_s = _moba_merge(mq8, mk8, mv8, kmean, pm, pls, pacc, dbsz, dseq, nb_past)
        xs = _out_ln(o_gdn_s.reshape(ms, WIDTH), o_moba_s.reshape(ms, WIDTH), xs, gate_s, w_out16[l], lng, lnb,
                     ms, zero_idx, alpha)
        fbuf = state_ffn_conv[l]
        prev1 = jnp.broadcast_to(fbuf[:, 1:2, :], (dbsz, dseq, d_ff)).reshape(ms, d_ff)
        prev2 = jnp.concatenate([fbuf, jnp.zeros((dbsz, dseq - 2, d_ff), F32)], axis=1).reshape(ms, d_ff)
        xs, gfull = _ffn(xs, fscale_s, fshift_s, fgate_s, w_up16[l], w_down16[l], ffn_conv_w[l], ffn_conv_b[l][None],
                         flng, flnb, ms, tc_ffn, zero_idx, alpha, prev_rows=(prev1, prev2), dec_seq=dseq)
        gcs = jnp.concatenate([state_gdn_conv[l], gs3[:, :, :3 * WIDTH]], axis=1)[:, dseq:, :]
        fcs = jnp.concatenate([fbuf, gfull.reshape(dbsz, dseq, d_ff)], axis=1)[:, dseq:, :]

        for lst, val in zip(outs, (mk.reshape(bsz, seq, nh, HEAD_DIM), mv.reshape(bsz, seq, nh, HEAD_DIM),
                                   mk3.reshape(dbsz, dseq, nh, HEAD_DIM), mv3.reshape(dbsz, dseq, nh, HEAD_DIM),
                                   s_p, s_s, gcp, gcs, fcp, fcs)):
            lst.append(val)

    return (xp.reshape(bsz, seq, d), xs.reshape(dbsz, dseq, d)) + tuple(jnp.stack(o) for o in outs)


def kernel(x_prompt, x_sample, cache_k, cache_v, state_gdn, state_gdn_conv, state_ffn_conv, page_table, c_prompt, c_sample, w_ada_mix, b_ada_mix, w_in, gdn_conv_w, gdn_a_log, gdn_dt_bias, gdn_norm_g, w_out, ln_mix_g, ln_mix_b, w_ada_ffn, b_ada_ffn, w_up, ffn_conv_w, ffn_conv_b, w_down, ln_ffn_g, ln_ffn_b):
    return _forward(x_prompt, x_sample, cache_k, cache_v, state_gdn, state_gdn_conv, state_ffn_conv, page_table,
                    c_prompt, c_sample, w_ada_mix, b_ada_mix, w_in, gdn_conv_w, gdn_a_log, gdn_dt_bias, gdn_norm_g,
                    w_out, ln_mix_g, ln_mix_b, w_ada_ffn, b_ada_ffn, w_up, ffn_conv_w, ffn_conv_b, w_down,
                    ln_ffn_g, ln_ffn_b, past_len=8192, tm_prompt=512, tm_in=1024, tn_in=512, tc_ffn=512, tt_gdn=256)
```

```python
import functools
import math

import jax
import jax.numpy as jnp
from jax import lax
from jax.experimental import pallas as pl
from jax.experimental.pallas import tpu as pltpu

F32 = jnp.float32
MXU_DTYPE = jnp.bfloat16

HEAD_DIM = 128
LANES = 128
SUBLANES = 8
N_HEADS = 8
WIDTH = N_HEADS * HEAD_DIM
GDN_CONV = 4
FFN_CONV = 3
MOBA_BLOCK = 256
MOBA_TOPK = 3
PAGE_SIZE = 128
ROPE_THETA = 500000.0
ROPE_DIM = HEAD_DIM // 4
LN_EPS = 1e-5
NORM_EPS = 1e-6
GDN_CHUNK = 128
INV_BLOCK = 16
VMEM_LIMIT = 56 * 1024 * 1024
ROW_SUBTILE = 256
NEG_INF = float("-inf")


def _cparams(*sem):
    return pltpu.CompilerParams(dimension_semantics=sem, vmem_limit_bytes=VMEM_LIMIT)


def _mm(a, b):
    return jnp.dot(a.astype(MXU_DTYPE), b.astype(MXU_DTYPE), preferred_element_type=F32)


def _mm_nt(a, b):
    return lax.dot_general(a.astype(MXU_DTYPE), b.astype(MXU_DTYPE), (((1,), (1,)), ((), ())),
                           preferred_element_type=F32)


def _mm_tn(a, b):
    return lax.dot_general(a.astype(MXU_DTYPE), b.astype(MXU_DTYPE), (((0,), (0,)), ((), ())),
                           preferred_element_type=F32)


def _silu(x):
    return x * jax.nn.sigmoid(x)


def _layer_norm_rows(r, g, b):
    mu = jnp.mean(r, axis=-1, keepdims=True)
    d = r - mu
    var = jnp.mean(d * d, axis=-1, keepdims=True)
    return d * lax.rsqrt(var + LN_EPS) * g + b


def _adaln_kernel(c_ref, w_ref, b_ref, o_ref):
    o_ref[...] = _mm(c_ref[...], w_ref[...]) + b_ref[...]


def _adaln(c, w, b):
    n_layers, d, n = w.shape
    rows = c.shape[0]
    tn = 1024
    return pl.pallas_call(
        _adaln_kernel,
        out_shape=jax.ShapeDtypeStruct((n_layers, rows, n), F32),
        grid=(n_layers, n // tn),
        in_specs=[pl.BlockSpec((rows, d), lambda l, j: (0, 0)),
                  pl.BlockSpec((None, d, tn), lambda l, j: (l, 0, j)),
                  pl.BlockSpec((None, 1, tn), lambda l, j: (l, 0, j))],
        out_specs=pl.BlockSpec((None, rows, tn), lambda l, j: (l, 0, j)),
        compiler_params=_cparams("arbitrary", "arbitrary"),
        name="adaln",
    )(c, w, b.reshape(n_layers, 1, n))


def _rope_store(y, c, s1, s2, o_ref):
    half = ROPE_DIM // 2
    for h in range(y.shape[1] // HEAD_DIM):
        yh = y[:, h * HEAD_DIM:(h + 1) * HEAD_DIM]
        o_ref[:, h * HEAD_DIM:(h + 1) * HEAD_DIM] = (
            yh * c + pltpu.roll(yh, HEAD_DIM - half, 1) * s1 + pltpu.roll(yh, half, 1) * s2)


def _in_proj_kernel(x_ref, sc_ref, sh_ref, w_ref, wba_ref, rc_ref, rs1_ref, rs2_ref,
                    g_ref, ba_ref, mq_ref, mk_ref, mv_ref, h_ref, *, n_g, n_m):
    j = pl.program_id(1)

    @pl.when(j == 0)
    def _():
        h_ref[...] = (x_ref[...] * (1.0 + sc_ref[...]) + sh_ref[...]).astype(h_ref.dtype)
        ba_ref[...] = jnp.dot(h_ref[...], wba_ref[...], preferred_element_type=F32)

    def proj():
        return jnp.dot(h_ref[...], w_ref[...], preferred_element_type=F32)

    @pl.when(j < n_g)
    def _():
        g_ref[...] = proj()

    @pl.when((j >= n_g) & (j < n_g + n_m))
    def _():
        _rope_store(proj(), rc_ref[...], rs1_ref[...], rs2_ref[...], mq_ref)

    @pl.when((j >= n_g + n_m) & (j < n_g + 2 * n_m))
    def _():
        _rope_store(proj(), rc_ref[...], rs1_ref[...], rs2_ref[...], mk_ref)

    @pl.when(j >= n_g + 2 * n_m)
    def _():
        mv_ref[...] = proj()


def _in_proj(x, scale, shift, w_all, w_ba, rope, tm, mod_index, rope_index):
    m, d = x.shape
    r = scale.shape[1]
    tn = w_all.shape[2]
    n_g, n_m = 4 * WIDTH // tn, WIDTH // tn
    mod_spec = pl.BlockSpec((None, r, d), lambda i, j: (mod_index(i), 0, 0))
    rope_spec = pl.BlockSpec((tm, LANES), lambda i, j: (rope_index(i), 0))

    def m_spec(first):
        return pl.BlockSpec((tm, tn), lambda i, j: (i, jnp.clip(j - first, 0, n_m - 1)))

    return pl.pallas_call(
        functools.partial(_in_proj_kernel, n_g=n_g, n_m=n_m),
        out_shape=(jax.ShapeDtypeStruct((m, 4 * WIDTH), F32),
                   jax.ShapeDtypeStruct((m, LANES), F32),
                   jax.ShapeDtypeStruct((m, WIDTH), F32),
                   jax.ShapeDtypeStruct((m, WIDTH), F32),
                   jax.ShapeDtypeStruct((m, WIDTH), F32)),
        grid=(m // tm, n_g + 3 * n_m),
        in_specs=[pl.BlockSpec((tm, d), lambda i, j: (i, 0)), mod_spec, mod_spec,
                  pl.BlockSpec((None, d, tn), lambda i, j: (j, 0, 0)),
                  pl.BlockSpec((d, LANES), lambda i, j: (0, 0)),
                  rope_spec, rope_spec, rope_spec],
        out_specs=(pl.BlockSpec((tm, tn), lambda i, j: (i, jnp.minimum(j, n_g - 1))),
                   pl.BlockSpec((tm, LANES), lambda i, j: (i, 0)),
                   m_spec(n_g), m_spec(n_g + n_m), m_spec(n_g + 2 * n_m)),
        scratch_shapes=[pltpu.VMEM((tm, d), MXU_DTYPE)],
        compiler_params=_cparams("arbitrary", "arbitrary"),
        name="in_proj",
    )(x, scale, shift, w_all, w_ba, *rope)


def _unit_lower_inverse(a_low, eye, same_blk):
    items = range(len(a_low))
    d = [jnp.where(same_blk, -a, 0.0) for a in a_low]
    o = [-a_low[i] - d[i] for i in items]
    x = [eye + d[i] for i in items]
    p = d
    for _ in range(int(math.log2(INV_BLOCK)) - 1):
        p = [_mm(p[i], p[i]) for i in items]
        x = [x[i] + _mm(x[i], p[i]) for i in items]
    y = [_mm(x[i], o[i]) for i in items]
    z = [eye + y[i] for i in items]
    p = y
    for _ in range(int(math.log2(GDN_CHUNK // INV_BLOCK)) - 1):
        p = [_mm(p[i], p[i]) for i in items]
        z = [z[i] + _mm(z[i], p[i]) for i in items]
    return z, x


def _gdn_wy(q, k, v, beta_c, eg_c, gc_col, gc_row, masks):
    eye, same_blk, tri_incl, tri_strict = masks
    items = range(len(q))
    c = q[0].shape[0]
    kb = [k[i] * beta_c[i] for i in items]
    p = [_mm_nt(jnp.concatenate([q[i], kb[i]], axis=0), k[i]) for i in items]
    decay = [jnp.exp(jnp.where(tri_incl, gc_col[i] - gc_row[i], NEG_INF)) for i in items]
    qk = [p[i][:c] * decay[i] for i in items]
    a_low = [jnp.where(tri_strict, p[i][c:] * decay[i], 0.0) for i in items]
    z, x = _unit_lower_inverse(a_low, eye, same_blk)
    rhs = [jnp.concatenate([v[i] * beta_c[i], kb[i] * eg_c[i]], axis=1) for i in items]
    t = [_mm(x[i], rhs[i]) for i in items]
    sol = [_mm(z[i], t[i]) for i in items]
    return [s[:, :HEAD_DIM] for s in sol], [s[:, HEAD_DIM:] for s in sol], qk


def _gdn_state_step(q, k, u, w, qk, eg_c, ek_c, gl, s):
    items = range(len(q))
    v_new = [u[i] - _mm(w[i], s[i]) for i in items]
    o = [_mm(q[i] * eg_c[i], s[i]) + _mm(qk[i], v_new[i]) for i in items]
    s_new = [s[i] * gl[i] + _mm_tn(k[i] * ek_c[i], v_new[i]) for i in items]
    return o, s_new


def _gdn_kernel(x_ref, ba_ref, cw_ref, alog_ref, dtb_ref, ng_ref, *rest, tt, n_valid, from_state):
    if from_state:
        cs_ref, s0_ref, o_ref, sout_ref, xbuf, zbuf, babuf, s_scr = rest
    else:
        o_ref, sout_ref, xbuf, s_scr = rest
    t = pl.program_id(1)
    nk = GDN_CONV - 1
    c = GDN_CHUNK

    if from_state:
        xbuf[...] = jnp.zeros(xbuf.shape, F32)
        xbuf[SUBLANES - nk:SUBLANES, :] = cs_ref[...]
        xbuf[SUBLANES:SUBLANES + n_valid, :] = x_ref[:, 0:3 * WIDTH]
        zbuf[...] = jnp.zeros(zbuf.shape, F32)
        zbuf[0:n_valid, :] = x_ref[:, 3 * WIDTH:4 * WIDTH]
        babuf[...] = jnp.zeros(babuf.shape, F32)
        babuf[0:n_valid, :] = ba_ref[...]
        s_scr[...] = s0_ref[...]
        ba = babuf[...]
    else:
        @pl.when(t == 0)
        def _():
            xbuf[0:SUBLANES, :] = jnp.zeros((SUBLANES, 3 * WIDTH), F32)
            s_scr[...] = jnp.zeros(s_scr.shape, F32)

        @pl.when(t > 0)
        def _():
            xbuf[0:SUBLANES, :] = xbuf[tt:tt + SUBLANES, :]

        xbuf[SUBLANES:SUBLANES + tt, :] = x_ref[:, 0:3 * WIDTH]
        ba = ba_ref[...]

    beta_all = jax.nn.sigmoid(ba)
    a_in = ba + dtb_ref[...]
    softplus = jnp.maximum(a_in, 0.0) + jnp.log1p(jnp.exp(-jnp.abs(a_in)))
    g_all = -jnp.exp(alog_ref[...]) * softplus
    if n_valid < tt:
        live = lax.broadcasted_iota(jnp.int32, (tt, LANES), 0) < n_valid
        beta_all = jnp.where(live, beta_all, 0.0)
        g_all = jnp.where(live, g_all, 0.0)

    row = lax.broadcasted_iota(jnp.int32, (c, c), 0)
    col = lax.broadcasted_iota(jnp.int32, (c, c), 1)
    tri_incl = row >= col
    tri_strict = row > col
    masks = (jnp.where(row == col, 1.0, 0.0).astype(F32), (row // INV_BLOCK) == (col // INV_BLOCK),
             tri_incl, tri_strict)
    ltri = jnp.where(tri_incl, 1.0, 0.0).astype(F32)

    def conv_act(col0):
        acc = xbuf[SUBLANES - nk:SUBLANES - nk + tt, col0:col0 + HEAD_DIM] * cw_ref[0:1, col0:col0 + HEAD_DIM]
        for i in range(1, GDN_CONV):
            acc = acc + (xbuf[SUBLANES - nk + i:SUBLANES - nk + i + tt, col0:col0 + HEAD_DIM]
                         * cw_ref[i:i + 1, col0:col0 + HEAD_DIM])
        return _silu(acc)

    def l2n(a):
        return a * lax.rsqrt(jnp.sum(a * a, axis=-1, keepdims=True) + NORM_EPS)

    chunk_decay = []
    for ci in range(tt // c):
        gcum = jnp.dot(ltri, g_all[ci * c:(ci + 1) * c, :], precision=lax.Precision.HIGHEST,
                       preferred_element_type=F32)
        chunk_decay.append((gcum, gcum.T, jnp.exp(gcum), jnp.exp(gcum[c - 1:c, :] - gcum)))

    heads = range(N_HEADS)
    n_chunks = tt // c
    q_t = [l2n(conv_act(h * HEAD_DIM)) * (HEAD_DIM ** -0.5) for h in heads]
    k_t = [l2n(conv_act(WIDTH + h * HEAD_DIM)) for h in heads]
    v_t = [conv_act(2 * WIDTH + h * HEAD_DIM) for h in heads]

    pairs = [(ci, h) for ci in range(n_chunks) for h in heads]
    rows = lambda a, ci: a[ci * c:(ci + 1) * c]
    lane = lambda a, h: a[:, N_HEADS + h:N_HEADS + h + 1]
    qs = [rows(q_t[h], ci) for ci, h in pairs]
    ks = [rows(k_t[h], ci) for ci, h in pairs]
    egs = [lane(chunk_decay[ci][2], h) for ci, h in pairs]
    u, w, qk = _gdn_wy(
        qs, ks, [rows(v_t[h], ci) for ci, h in pairs],
        [rows(beta_all, ci)[:, h:h + 1] for ci, h in pairs], egs,
        [lane(chunk_decay[ci][0], h) for ci, h in pairs],
        [chunk_decay[ci][1][N_HEADS + h:N_HEADS + h + 1, :] for ci, h in pairs], masks)

    s = [s_scr[h] for h in heads]
    for ci in range(n_chunks):
        sl = slice(ci * N_HEADS, (ci + 1) * N_HEADS)
        eks = [lane(chunk_decay[ci][3], h) for h in heads]
        gls = [lane(chunk_decay[ci][2], h)[c - 1:c] for h in heads]
        o, s = _gdn_state_step(qs[sl], ks[sl], u[sl], w[sl], qk[sl], egs[sl], eks, gls, s)
        r0 = ci * c
        for h in heads:
            hs = slice(h * HEAD_DIM, (h + 1) * HEAD_DIM)
            if from_state:
                zh = zbuf[r0:r0 + c, hs]
            else:
                zh = x_ref[r0:r0 + c, 3 * WIDTH + h * HEAD_DIM:3 * WIDTH + (h + 1) * HEAD_DIM]
            on = o[h] * lax.rsqrt(jnp.mean(o[h] * o[h], axis=-1, keepdims=True) + NORM_EPS) * ng_ref[...] * _silu(zh)
            if from_state:
                o_ref[:, hs] = on[0:n_valid].astype(o_ref.dtype)
            else:
                o_ref[r0:r0 + c, hs] = on.astype(o_ref.dtype)
    for h in heads:
        s_scr[h] = s[h]
        sout_ref[h] = s[h]


def _gdn_prompt(g, ba, conv_w, alog_pad, dtb_pad, norm_g, batch, seq, tt):
    m = g.shape[0]
    nt = seq // tt
    kern = functools.partial(_gdn_kernel, tt=tt, n_valid=tt, from_state=False)
    small = lambda shape: pl.BlockSpec(shape, lambda b, t: (0, 0))
    return pl.pallas_call(
        kern,
        out_shape=(jax.ShapeDtypeStruct((m, WIDTH), MXU_DTYPE),
                   jax.ShapeDtypeStruct((batch, N_HEADS, HEAD_DIM, HEAD_DIM), F32)),
        grid=(batch, nt),
        in_specs=[pl.BlockSpec((tt, 4 * WIDTH), lambda b, t: (b * nt + t, 0)),
                  pl.BlockSpec((tt, LANES), lambda b, t: (b * nt + t, 0)),
                  small((GDN_CONV, 3 * WIDTH)), small((1, LANES)), small((1, LANES)), small((1, HEAD_DIM))],
        out_specs=(pl.BlockSpec((tt, WIDTH), lambda b, t: (b * nt + t, 0)),
                   pl.BlockSpec((None, N_HEADS, HEAD_DIM, HEAD_DIM), lambda b, t: (b, 0, 0, 0))),
        scratch_shapes=[pltpu.VMEM((tt + SUBLANES, 3 * WIDTH), F32),
                        pltpu.VMEM((N_HEADS, HEAD_DIM, HEAD_DIM), F32)],
        compiler_params=_cparams("arbitrary", "arbitrary"),
        name="gdn_prompt",
    )(g, ba, conv_w, alog_pad, dtb_pad, norm_g)


def _gdn_sample(g, ba, conv_state, s0, conv_w, alog_pad, dtb_pad, norm_g, batch, dec_seq):
    tt = GDN_CHUNK
    kern = functools.partial(_gdn_kernel, tt=tt, n_valid=dec_seq, from_state=True)
    small = lambda shape: pl.BlockSpec(shape, lambda b, t: (0, 0))
    return pl.pallas_call(
        kern,
        out_shape=(jax.ShapeDtypeStruct((batch, dec_seq, WIDTH), F32),
                   jax.ShapeDtypeStruct((batch, N_HEADS, HEAD_DIM, HEAD_DIM), F32)),
        grid=(batch, 1),
        in_specs=[pl.BlockSpec((None, dec_seq, 4 * WIDTH), lambda b, t: (b, 0, 0)),
                  pl.BlockSpec((None, dec_seq, LANES), lambda b, t: (b, 0, 0)),
                  small((GDN_CONV, 3 * WIDTH)), small((1, LANES)), small((1, LANES)), small((1, HEAD_DIM)),
                  pl.BlockSpec((None, GDN_CONV - 1, 3 * WIDTH), lambda b, t: (b, 0, 0)),
                  pl.BlockSpec((None, N_HEADS, HEAD_DIM, HEAD_DIM), lambda b, t: (b, 0, 0, 0))],
        out_specs=(pl.BlockSpec((None, dec_seq, WIDTH), lambda b, t: (b, 0, 0)),
                   pl.BlockSpec((None, N_HEADS, HEAD_DIM, HEAD_DIM), lambda b, t: (b, 0, 0, 0))),
        scratch_shapes=[pltpu.VMEM((tt + SUBLANES, 3 * WIDTH), F32),
                        pltpu.VMEM((tt, WIDTH), F32),
                        pltpu.VMEM((tt, LANES), F32),
                        pltpu.VMEM((N_HEADS, HEAD_DIM, HEAD_DIM), F32)],
        compiler_params=_cparams("arbitrary", "arbitrary"),
        name="gdn_sample",
    )(g, ba, conv_w, alog_pad, dtb_pad, norm_g, conv_state, s0)


def _topk_select(gate, n_valid, idx, axis):
    n = gate.shape[axis]
    cnt = jnp.zeros(gate.shape, F32)
    for j in range(n):
        cand = lax.slice_in_dim(gate, j, j + 1, axis=axis)
        beats = jnp.where(cand > gate, 1.0, jnp.where((cand == gate) & (j < idx), 1.0, 0.0))
        cnt = cnt + jnp.where(j < n_valid, beats, 0.0)
    return (cnt < MOBA_TOPK) & (idx < n_valid)


MOBA_HEADS_PER_STEP = 2


def _moba_prompt_kernel(q_ref, k_ref, v_ref, o_ref, k16, vt16, kmean, *, nb):
    i = pl.program_id(2)
    blk = MOBA_BLOCK
    scale = HEAD_DIM ** -0.5
    group = range(MOBA_HEADS_PER_STEP)
    lanes = lambda g: slice(g * HEAD_DIM, (g + 1) * HEAD_DIM)

    @pl.when(i == 0)
    def _():
        for g in group:
            for j in range(nb):
                kj = k_ref[j * blk:(j + 1) * blk, lanes(g)]
                k16[g, j] = kj.astype(MXU_DTYPE)
                kmean[g, j:j + 1, :] = jnp.mean(kj, axis=0, keepdims=True)
                vt16[g, j] = v_ref[j * blk:(j + 1) * blk, lanes(g)].T.astype(MXU_DTYPE)

    kpos = lax.broadcasted_iota(jnp.int32, (blk, blk), 0)
    qpos = lax.broadcasted_iota(jnp.int32, (blk, blk), 1)

    def attend(cur):
        qt = [q_ref[:, lanes(g)].T for g in group]
        qt16 = [a.astype(MXU_DTYPE) for a in qt]
        if cur > MOBA_TOPK:
            gate = [jnp.dot(kmean[g], qt[g], precision=lax.Precision.HIGHEST, preferred_element_type=F32)
                    for g in group]
            blk_idx = lax.broadcasted_iota(jnp.int32, gate[0].shape, 0)
            sel = [jnp.where(_topk_select(gate[g], cur, blk_idx, 0), 1.0, 0.0) for g in group]
        s = [[] for _ in group]
        for j in range(cur + 1):
            for g in group:
                sj = jnp.dot(k16[g, j], qt16[g], preferred_element_type=F32) * scale
                if j == cur:
                    sj = jnp.where(kpos <= qpos, sj, NEG_INF)
                elif cur > MOBA_TOPK:
                    sj = jnp.where(sel[g][j:j + 1, :] > 0.5, sj, NEG_INF)
                s[g].append(sj)
        m = [functools.reduce(jnp.maximum, [jnp.max(sj, axis=0, keepdims=True) for sj in s[g]]) for g in group]
        l = [jnp.zeros((1, blk), F32) for _ in group]
        acc = [jnp.zeros((HEAD_DIM, blk), F32) for _ in group]
        for j in range(cur + 1):
            for g in group:
                p = jnp.exp(s[g][j] - m[g])
                l[g] = l[g] + jnp.sum(p, axis=0, keepdims=True)
                acc[g] = acc[g] + jnp.dot(vt16[g, j], p.astype(MXU_DTYPE), preferred_element_type=F32)
        for g in group:
            o_ref[:, lanes(g)] = (acc[g] / l[g]).T.astype(o_ref.dtype)

    for cur in range(nb):
        pl.when(i == cur)(functools.partial(attend, cur))


def _moba_prompt(mq, mk, mv, batch, seq):
    m = mq.shape[0]
    nb = seq // MOBA_BLOCK
    hps = MOBA_HEADS_PER_STEP
    kern = functools.partial(_moba_prompt_kernel, nb=nb)
    kv_spec = pl.BlockSpec((seq, hps * HEAD_DIM), lambda b, h, i: (b, h))
    q_spec = pl.BlockSpec((MOBA_BLOCK, hps * HEAD_DIM), lambda b, h, i: (b * nb + i, h))
    return pl.pallas_call(
        kern,
        out_shape=jax.ShapeDtypeStruct((m, WIDTH), MXU_DTYPE),
        grid=(batch, N_HEADS // hps, nb),
        in_specs=[q_spec, kv_spec, kv_spec],
        out_specs=q_spec,
        scratch_shapes=[pltpu.VMEM((hps, nb, MOBA_BLOCK, HEAD_DIM), MXU_DTYPE),
                        pltpu.VMEM((hps, nb, HEAD_DIM, MOBA_BLOCK), MXU_DTYPE),
                        pltpu.VMEM((hps, nb, HEAD_DIM), F32)],
        compiler_params=_cparams("arbitrary", "arbitrary", "arbitrary"),
        name="moba_prompt",
    )(mq, mk, mv)


QPAD = SUBLANES


PAST_BLOCKS_PER_STEP = 4


def _moba_partial_kernel(pt_ref, q_ref, *refs):
    nblk, ppb = PAST_BLOCKS_PER_STEP, MOBA_BLOCK // PAGE_SIZE
    k_refs, v_refs = refs[:nblk * ppb], refs[nblk * ppb:2 * nblk * ppb]
    kmean_ref, pm_ref, pl_ref, pacc_ref = refs[2 * nblk * ppb:]
    n = pl.program_id(1)
    scale = HEAD_DIM ** -0.5
    rows = PAGE_SIZE * N_HEADS
    q16 = q_ref[...].astype(MXU_DTYPE)
    q_head = lax.broadcasted_iota(jnp.int32, (N_HEADS * QPAD, rows), 0) // QPAD
    k_head = lax.broadcasted_iota(jnp.int32, (N_HEADS * QPAD, rows), 1) % N_HEADS
    same_head = q_head == k_head
    for g in range(nblk):
        ks = [k_refs[g * ppb + pg][...] for pg in range(ppb)]
        kmean_ref[n * nblk + g] = sum(jnp.sum(k, axis=0) for k in ks) * (1.0 / MOBA_BLOCK)
        s = [jnp.where(same_head, _mm_nt(q16, k.reshape(rows, HEAD_DIM)) * scale, NEG_INF) for k in ks]
        mx = functools.reduce(jnp.maximum, [jnp.max(sp, axis=-1, keepdims=True) for sp in s])
        p = [jnp.exp(sp - mx) for sp in s]
        lsum = sum(jnp.sum(pp, axis=-1, keepdims=True) for pp in p)
        pm_ref[g] = jnp.broadcast_to(mx, pm_ref.shape[1:])
        pl_ref[g] = jnp.broadcast_to(lsum, pl_ref.shape[1:])
        pacc_ref[g] = sum(_mm(p[pg], v_refs[g * ppb + pg][...].reshape(rows, HEAD_DIM)) for pg in range(ppb))


def _moba_partials(page_table_flat, q, cache_k, cache_v, layer, batch, nb, n_pages):
    nblk, ppb = PAST_BLOCKS_PER_STEP, MOBA_BLOCK // PAGE_SIZE
    assert nb % nblk == 0
    hq = N_HEADS * QPAD

    def page_spec(g, pg):
        return pl.BlockSpec((None, None, PAGE_SIZE, N_HEADS, HEAD_DIM),
                            lambda b, n, pt: (layer, pt[b * n_pages + ppb * (n * nblk + g) + pg], 0, 0, 0))

    page_specs = [page_spec(g, pg) for g in range(nblk) for pg in range(ppb)]
    part_shape = jax.ShapeDtypeStruct((batch, nb, hq, LANES), F32)
    part_spec = pl.BlockSpec((None, nblk, hq, LANES), lambda b, n, pt: (b, n, 0, 0))
    return pl.pallas_call(
        _moba_partial_kernel,
        out_shape=(jax.ShapeDtypeStruct((batch, nb, N_HEADS, HEAD_DIM), F32), part_shape, part_shape, part_shape),
        grid_spec=pltpu.PrefetchScalarGridSpec(
            num_scalar_prefetch=1,
            grid=(batch, nb // nblk),
            in_specs=[pl.BlockSpec((None, hq, HEAD_DIM), lambda b, n, pt: (b, 0, 0))] + page_specs + page_specs,
            out_specs=(pl.BlockSpec((None, nb, N_HEADS, HEAD_DIM), lambda b, n, pt: (b, 0, 0, 0)),
                       part_spec, part_spec, part_spec)),
        compiler_params=_cparams("arbitrary", "arbitrary"),
        name="moba_sample_partials",
    )(page_table_flat, q, *([cache_k] * (nblk * ppb)), *([cache_v] * (nblk * ppb)))


def _moba_merge_kernel(q_ref, k_ref, v_ref, kmean_ref, pm_ref, pl_ref, pacc_ref, o_ref, *, dec_seq, nb):
    scale = HEAD_DIM ** -0.5
    q8, k8, v8 = q_ref[...], k_ref[...], v_ref[...]
    qi = lax.broadcasted_iota(jnp.int32, (QPAD, QPAD), 0)
    kj = lax.broadcasted_iota(jnp.int32, (QPAD, QPAD), 1)
    own_ok = (kj <= qi) & (kj < dec_seq)
    blk_idx = lax.broadcasted_iota(jnp.int32, (QPAD, nb), 1)
    for h in range(N_HEADS):
        hs = slice(h * HEAD_DIM, (h + 1) * HEAD_DIM)
        hq = slice(h * QPAD, (h + 1) * QPAD)
        gate = lax.dot_general(q8[:, hs], kmean_ref[:, h, :], (((1,), (1,)), ((), ())),
                               precision=lax.Precision.HIGHEST, preferred_element_type=F32)
        sel = jnp.where(_topk_select(gate, nb, blk_idx, 1), 1.0, 0.0)
        s_own = jnp.where(own_ok, _mm_nt(q8[:, hs], k8[:, hs]) * scale, NEG_INF)
        m_tot = jnp.broadcast_to(jnp.max(s_own, axis=-1, keepdims=True), (QPAD, LANES))
        for n in range(nb):
            m_tot = jnp.maximum(m_tot, jnp.where(sel[:, n:n + 1] > 0.5, pm_ref[n, hq, :], NEG_INF))
        p_own = jnp.exp(s_own - m_tot[:, 0:1])
        l_tot = jnp.broadcast_to(jnp.sum(p_own, axis=-1, keepdims=True), (QPAD, LANES))
        acc = jnp.zeros((QPAD, HEAD_DIM), F32)
        for j in range(dec_seq):
            acc = acc + p_own[:, j:j + 1] * v8[j:j + 1, hs]
        for n in range(nb):
            w = jnp.where(sel[:, n:n + 1] > 0.5, jnp.exp(pm_ref[n, hq, :] - m_tot), 0.0)
            l_tot = l_tot + w * pl_ref[n, hq, :]
            acc = acc + w * pacc_ref[n, hq, :]
        o_ref[:, hs] = (acc / l_tot)[0:dec_seq].astype(o_ref.dtype)


def _moba_merge(q, k, v, kmean, pm, plsum, pacc, batch, dec_seq, nb):
    kern = functools.partial(_moba_merge_kernel, dec_seq=dec_seq, nb=nb)
    row_spec = pl.BlockSpec((None, QPAD, WIDTH), lambda b: (b, 0, 0))
    part_spec = pl.BlockSpec((None, nb, N_HEADS * QPAD, LANES), lambda b: (b, 0, 0, 0))
    return pl.pallas_call(
        kern,
        out_shape=jax.ShapeDtypeStruct((batch, dec_seq, WIDTH), F32),
        grid=(batch,),
        in_specs=[row_spec, row_spec, row_spec,
                  pl.BlockSpec((None, nb, N_HEADS, HEAD_DIM), lambda b: (b, 0, 0, 0)),
                  part_spec, part_spec, part_spec],
        out_specs=pl.BlockSpec((None, dec_seq, WIDTH), lambda b: (b, 0, 0)),
        compiler_params=_cparams("arbitrary"),
        name="moba_sample_merge",
    )(q, k, v, kmean, pm, plsum, pacc)


def _out_ln_kernel(og_ref, om_ref, x_ref, gate_ref, w1_ref, w2_ref, lng_ref, lnb_ref, o_ref, *, alpha):
    tm = x_ref.shape[0]
    rs = min(tm, ROW_SUBTILE)
    subs = [slice(s * rs, (s + 1) * rs) for s in range(tm // rs)]
    y = [_mm(og_ref[sl, :], w1_ref[...]) + _mm(om_ref[sl, :], w2_ref[...]) for sl in subs]
    for sl, ys in zip(subs, y):
        gate = gate_ref[...] if gate_ref.shape[0] == 1 else gate_ref[sl, :]
        r = alpha * x_ref[sl, :] + (1.0 + gate) * ys
        o_ref[sl, :] = _layer_norm_rows(r, lng_ref[...], lnb_ref[...])


def _out_ln(o_gdn, o_moba, x, gate, w_out, ln_g, ln_b, tm, mod_index, alpha):
    m, d = x.shape
    r = gate.shape[1]
    vec = pl.BlockSpec((1, d), lambda i: (0, 0))
    return pl.pallas_call(
        functools.partial(_out_ln_kernel, alpha=alpha),
        out_shape=jax.ShapeDtypeStruct((m, d), F32),
        grid=(m // tm,),
        in_specs=[pl.BlockSpec((tm, WIDTH), lambda i: (i, 0)), pl.BlockSpec((tm, WIDTH), lambda i: (i, 0)),
                  pl.BlockSpec((tm, d), lambda i: (i, 0)),
                  pl.BlockSpec((None, r, d), lambda i: (mod_index(i), 0, 0)),
                  pl.BlockSpec((WIDTH, d), lambda i: (0, 0)), pl.BlockSpec((WIDTH, d), lambda i: (1, 0)),
                  vec, vec],
        out_specs=pl.BlockSpec((tm, d), lambda i: (i, 0)),
        compiler_params=_cparams("arbitrary"),
        name="out_proj_ln",
    )(o_gdn, o_moba, x, gate, w_out, w_out, ln_g, ln_b)


def _ffn_kernel(x_ref, sc_ref, sh_ref, gate_ref, wg_ref, wu_ref, wd_ref, cw_ref, cb_ref, lng_ref, lnb_ref, *rest,
                tm, n_sub, n_chunks, tiles_per_seq, dec_seq, alpha):
    sample = dec_seq is not None
    if sample:
        p1_ref, p2_ref, o_ref, st_ref, h_ref, acc_ref, gbuf = rest
    else:
        o_ref, st_ref, h_ref, acc_ref, gbuf, carry = rest
    i = pl.program_id(0)
    c = pl.program_id(1)

    @pl.when(c == 0)
    def _():
        h_ref[...] = (x_ref[...] * (1.0 + sc_ref[...]) + sh_ref[...]).astype(h_ref.dtype)
        acc_ref[...] = jnp.zeros(acc_ref.shape, F32)

    tc = wg_ref.shape[1]
    rs = tm // n_sub
    subs = [slice(s * rs, (s + 1) * rs) for s in range(n_sub)]
    if sample:
        gbuf[0:SUBLANES, :] = jnp.zeros((SUBLANES, tc), F32)
    else:
        first = (i % tiles_per_seq) == 0

        @pl.when(first)
        def _():
            gbuf[0:SUBLANES, :] = jnp.zeros((SUBLANES, tc), F32)

        @pl.when(jnp.logical_not(first))
        def _():
            gbuf[0:SUBLANES, :] = carry[c]

    g, u = [], []
    for s, sl in enumerate(subs):
        g.append(jnp.dot(h_ref[sl, :], wg_ref[...], preferred_element_type=F32))
        gbuf[SUBLANES + s * rs:SUBLANES + (s + 1) * rs, :] = g[s]
        u.append(jnp.dot(h_ref[sl, :], wu_ref[...], preferred_element_type=F32))
    if not sample:
        carry[c] = g[-1][rs - SUBLANES:rs, :]
        st_ref[...] = g[-1][rs - (FFN_CONV - 1):rs, :]
    for s, sl in enumerate(subs):
        prev1 = gbuf[SUBLANES - 1 + s * rs:SUBLANES - 1 + (s + 1) * rs, :]
        prev2 = gbuf[SUBLANES - 2 + s * rs:SUBLANES - 2 + (s + 1) * rs, :]
        if sample:
            pos = lax.broadcasted_iota(jnp.int32, (rs, tc), 0) % dec_seq
            prev1 = jnp.where(pos >= 1, prev1, p1_ref[sl, :])
            prev2 = jnp.where(pos >= 2, prev2, p2_ref[sl, :])
            st_ref[sl, :] = g[s]
        conv = prev2 * cw_ref[0:1, :] + prev1 * cw_ref[1:2, :] + g[s] * cw_ref[2:3, :]
        act = _silu(conv + cb_ref[...]) * u[s]
        acc_ref[sl, :] += jnp.dot(act.astype(MXU_DTYPE), wd_ref[...], preferred_element_type=F32)

    @pl.when(c == n_chunks - 1)
    def _():
        r = alpha * x_ref[...] + (1.0 + gate_ref[...]) * acc_ref[...]
        o_ref[...] = _layer_norm_rows(r, lng_ref[...], lnb_ref[...])


def _ffn(x, scale, shift, gate, w_up, w_down, conv_w, conv_b, ln_g, ln_b, tm, mod_index, alpha,
         tiles_per_seq=None, prev_rows=None, dec_seq=None):
    m, d = x.shape
    d_ff = w_down.shape[0]
    tc = w_up.shape[2]
    n_chunks = d_ff // tc
    r = scale.shape[1]
    sample = prev_rows is not None
    mod_spec = pl.BlockSpec((None, r, d), lambda i, c: (mod_index(i), 0, 0))
    vec = pl.BlockSpec((1, d), lambda i, c: (0, 0))
    in_specs = [pl.BlockSpec((tm, d), lambda i, c: (i, 0)), mod_spec, mod_spec, mod_spec,
                pl.BlockSpec((None, d, tc), lambda i, c: (c, 0, 0)),
                pl.BlockSpec((None, d, tc), lambda i, c: (n_chunks + c, 0, 0)),
                pl.BlockSpec((tc, d), lambda i, c: (c, 0)),
                pl.BlockSpec((FFN_CONV, tc), lambda i, c: (0, c)),
                pl.BlockSpec((1, tc), lambda i, c: (0, c)),
                vec, vec]
    args = [x, scale, shift, gate, w_up, w_up, w_down, conv_w, conv_b, ln_g, ln_b]
    scratch = [pltpu.VMEM((tm, d), MXU_DTYPE), pltpu.VMEM((tm, d), F32), pltpu.VMEM((tm + SUBLANES, tc), F32)]
    if sample:
        in_specs += [pl.BlockSpec((tm, tc), lambda i, c: (i, c))] * 2
        args += list(prev_rows)
        st_shape = jax.ShapeDtypeStruct((m, d_ff), F32)
        st_spec = pl.BlockSpec((tm, tc), lambda i, c: (i, c))
    else:
        st_shape = jax.ShapeDtypeStruct((m // tm, FFN_CONV - 1, d_ff), F32)
        st_spec = pl.BlockSpec((None, FFN_CONV - 1, tc), lambda i, c: (i, 0, c))
        scratch.append(pltpu.VMEM((n_chunks, SUBLANES, tc), F32))
    kern = functools.partial(_ffn_kernel, tm=tm, n_sub=max(1, tm // ROW_SUBTILE), n_chunks=n_chunks,
                             tiles_per_seq=tiles_per_seq,
                             dec_seq=dec_seq if sample else None, alpha=alpha)
    return pl.pallas_call(
        kern,
        out_shape=(jax.ShapeDtypeStruct((m, d), F32), st_shape),
        grid=(m // tm, n_chunks),
        in_specs=in_specs,
        out_specs=(pl.BlockSpec((tm, d), lambda i, c: (i, 0)), st_spec),
        scratch_shapes=scratch,
        compiler_params=_cparams("arbitrary", "arbitrary"),
        name="ffn_sample" if sample else "ffn_prompt",
    )(*args)


def _rope_tables(pos):
    half = ROPE_DIM // 2
    inv_freq = ROPE_THETA ** (-jnp.arange(half, dtype=F32) / half)
    ang = pos.astype(F32)[:, None] * inv_freq[None, :]
    cos, sin = jnp.cos(ang), jnp.sin(ang)
    n = pos.shape[0]
    ones = jnp.ones((n, HEAD_DIM - ROPE_DIM), F32)
    zeros = jnp.zeros((n, HEAD_DIM - ROPE_DIM), F32)
    zh = jnp.zeros((n, half), F32)
    return (jnp.concatenate([cos, cos, ones], axis=1),
            jnp.concatenate([-sin, zh, zeros], axis=1),
            jnp.concatenate([zh, sin, zeros], axis=1))


def _forward(x_prompt, x_sample, cache_k, cache_v, state_gdn, state_gdn_conv, state_ffn_conv, page_table,
             c_prompt, c_sample, w_ada_mix, b_ada_mix, w_in, gdn_conv_w, gdn_a_log, gdn_dt_bias, gdn_norm_g,
             w_out, ln_mix_g, ln_mix_b, w_ada_ffn, b_ada_ffn, w_up, ffn_conv_w, ffn_conv_b, w_down,
             ln_ffn_g, ln_ffn_b, *, past_len, tm_prompt, tm_in, tn_in, tc_ffn, tt_gdn):
    bsz, seq, d = x_prompt.shape
    dbsz, dseq, _ = x_sample.shape
    depth = w_in.shape[0]
    d_ff = w_down.shape[1]
    alpha = (2 * depth) ** 0.25
    n_pool = cache_k.shape[1]
    n_pages = page_table.shape[1]
    nb_past = past_len // MOBA_BLOCK
    assert d == 2 * WIDTH and past_len % MOBA_BLOCK == 0 and nb_past >= MOBA_TOPK
    assert n_pages * PAGE_SIZE == past_len and seq % MOBA_BLOCK == 0
    mp, ms = bsz * seq, dbsz * dseq
    tps = seq // tm_prompt

    gw, nh = WIDTH, N_HEADS

    def col_blocks(w, block):
        return w.reshape(depth, d, w.shape[2] // block, block).transpose(0, 2, 1, 3).astype(MXU_DTYPE)

    w_main = col_blocks(jnp.concatenate([w_in[:, :, :4 * gw], w_in[:, :, 4 * gw + 2 * nh:]], axis=2), tn_in)
    w_ba = jnp.pad(w_in[:, :, 4 * gw:4 * gw + 2 * nh], ((0, 0), (0, 0), (0, LANES - 2 * nh))).astype(MXU_DTYPE)
    w_up16 = col_blocks(w_up, tc_ffn)
    w_out16, w_down16 = w_out.astype(MXU_DTYPE), w_down.astype(MXU_DTYPE)
    alog_pad = jnp.pad(gdn_a_log, ((0, 0), (nh, LANES - 2 * nh)))[:, None, :]
    dtb_pad = jnp.pad(gdn_dt_bias, ((0, 0), (nh, LANES - 2 * nh)))[:, None, :]

    c_all = jnp.concatenate([c_prompt, c_sample], axis=0)
    mod_mix = _adaln(c_all, w_ada_mix, b_ada_mix)
    mod_ffn = _adaln(c_all, w_ada_ffn, b_ada_ffn)

    def mods(mod, l):
        out_p, out_s = [], []
        for part in range(3):
            v = mod[l, :, part * d:(part + 1) * d]
            out_p.append(v[:bsz, None, :])
            out_s.append(jnp.repeat(v[bsz:], dseq, axis=0)[None])
        return out_p, out_s

    rope_p = _rope_tables(jnp.arange(seq))
    rope_s = _rope_tables(jnp.tile(past_len + jnp.arange(dseq), dbsz))
    pt_flat = page_table.reshape(-1)

    xp = x_prompt.reshape(mp, d)
    xs = x_sample.reshape(ms, d)
    p_idx = lambda i: i // tps
    zero_idx = lambda i: 0
    outs = [[] for _ in range(10)]
    for l in range(depth):
        (shift_p, scale_p, gate_p), (shift_s, scale_s, gate_s) = mods(mod_mix, l)
        (fshift_p, fscale_p, fgate_p), (fshift_s, fscale_s, fgate_s) = mods(mod_ffn, l)
        lng, lnb = ln_mix_g[l][None], ln_mix_b[l][None]
        flng, flnb = ln_ffn_g[l][None], ln_ffn_b[l][None]
        ng = gdn_norm_g[l][None]

        tps_in = seq // tm_in
        g, ba, mq, mk, mv = _in_proj(xp, scale_p, shift_p, w_main[l], w_ba[l], rope_p, tm_in,
                                     lambda i: i // tps_in, lambda i: i % tps_in)
        o_gdn, s_p = _gdn_prompt(g, ba, gdn_conv_w[l], alog_pad[l], dtb_pad[l], ng, bsz, seq, tt_gdn)
        o_moba = _moba_prompt(mq, mk, mv, bsz, seq)
        xp = _out_ln(o_gdn, o_moba, xp, gate_p, w_out16[l], lng, lnb, tm_prompt, p_idx, alpha)
        xp, fcp = _ffn(xp, fscale_p, fshift_p, fgate_p, w_up16[l], w_down16[l], ffn_conv_w[l], ffn_conv_b[l][None],
                       flng, flnb, tm_prompt, p_idx, alpha, tiles_per_seq=tps)
        fcp = fcp[tps - 1::tps]
        gcp = g.reshape(bsz, seq, 4 * WIDTH)[:, seq - (GDN_CONV - 1):, :3 * WIDTH]

        gs, bas, mqs, mks, mvs = _in_proj(xs, scale_s, shift_s, w_main[l], w_ba[l], rope_s, ms,
                                          zero_idx, zero_idx)
        gs3 = gs.reshape(dbsz, dseq, 4 * WIDTH)
        o_gdn_s, s_s = _gdn_sample(gs3, bas.reshape(dbsz, dseq, LANES), state_gdn_conv[l], state_gdn[l],
                                   gdn_conv_w[l], alog_pad[l], dtb_pad[l], ng, dbsz, dseq)
        mq3, mk3, mv3 = (a.reshape(dbsz, dseq, WIDTH) for a in (mqs, mks, mvs))
        mq8, mk8, mv8 = (jnp.pad(a, ((0, 0), (0, QPAD - dseq), (0, 0))) for a in (mq3, mk3, mv3))
        mq_hm = mq8.reshape(dbsz, QPAD, nh, HEAD_DIM).transpose(0, 2, 1, 3).reshape(dbsz, nh * QPAD, HEAD_DIM)
        kmean, pm, pls, pacc = _moba_partials(pt_flat, mq_hm, cache_k, cache_v, l, dbsz, nb_past, n_pages)
        o_moba_s = _moba_merge(mq8, mk8, mv8, kmean, pm, pls, pacc, dbsz, dseq, nb_past)
        xs = _out_ln(o_gdn_s.reshape(ms, WIDTH), o_moba_s.reshape(ms, WIDTH), xs, gate_s, w_out16[l], lng, lnb,
                     ms, zero_idx, alpha)
        fbuf = state_ffn_conv[l]
        prev1 = jnp.broadcast_to(fbuf[:, 1:2, :], (dbsz, dseq, d_ff)).reshape(ms, d_ff)
        prev2 = jnp.concatenate([fbuf, jnp.zeros((dbsz, dseq - 2, d_ff), F32)], axis=1).reshape(ms, d_ff)
        xs, gfull = _ffn(xs, fscale_s, fshift_s, fgate_s, w_up16[l], w_down16[l], ffn_conv_w[l], ffn_conv_b[l][None],
                         flng, flnb, ms, zero_idx, alpha, prev_rows=(prev1, prev2), dec_seq=dseq)
        gcs = jnp.concatenate([state_gdn_conv[l], gs3[:, :, :3 * WIDTH]], axis=1)[:, dseq:, :]
        fcs = jnp.concatenate([fbuf, gfull.reshape(dbsz, dseq, d_ff)], axis=1)[:, dseq:, :]

        for lst, val in zip(outs, (mk.reshape(bsz, seq, nh, HEAD_DIM), mv.reshape(bsz, seq, nh, HEAD_DIM),
                                   mk3.reshape(dbsz, dseq, nh, HEAD_DIM), mv3.reshape(dbsz, dseq, nh, HEAD_DIM),
                                   s_p, s_s, gcp, gcs, fcp, fcs)):
            lst.append(val)

    return (xp.reshape(bsz, seq, d), xs.reshape(dbsz, dseq, d)) + tuple(jnp.stack(o) for o in outs)


def kernel(x_prompt, x_sample, cache_k, cache_v, state_gdn, state_gdn_conv, state_ffn_conv, page_table, c_prompt, c_sample, w_ada_mix, b_ada_mix, w_in, gdn_conv_w, gdn_a_log, gdn_dt_bias, gdn_norm_g, w_out, ln_mix_g, ln_mix_b, w_ada_ffn, b_ada_ffn, w_up, ffn_conv_w, ffn_conv_b, w_down, ln_ffn_g, ln_ffn_b):
    return _forward(x_prompt, x_sample, cache_k, cache_v, state_gdn, state_gdn_conv, state_ffn_conv, page_table,
                    c_prompt, c_sample, w_ada_mix, b_ada_mix, w_in, gdn_conv_w, gdn_a_log, gdn_dt_bias, gdn_norm_g,
                    w_out, ln_mix_g, ln_mix_b, w_ada_ffn, b_ada_ffn, w_up, ffn_conv_w, ffn_conv_b, w_down,
                    ln_ffn_g, ln_ffn_b, past_len=8192, tm_prompt=512, tm_in=1024, tn_in=512, tc_ffn=512, tt_gdn=256)
```

```python
import functools
import math

import jax
import jax.numpy as jnp
from jax import lax
from jax.experimental import pallas as pl
from jax.experimental.pallas import tpu as pltpu

F32 = jnp.float32
MXU_DTYPE = jnp.bfloat16

HEAD_DIM = 128
LANES = 128
SUBLANES = 8
N_HEADS = 8
WIDTH = N_HEADS * HEAD_DIM
GDN_CONV = 4
FFN_CONV = 3
MOBA_BLOCK = 256
MOBA_TOPK = 3
PAGE_SIZE = 128
ROPE_THETA = 500000.0
ROPE_DIM = HEAD_DIM // 4
LN_EPS = 1e-5
NORM_EPS = 1e-6
GDN_CHUNK = 128
INV_BLOCK = 16
VMEM_LIMIT = 56 * 1024 * 1024
ROW_SUBTILE = 256
NEG_INF = float("-inf")


def _cparams(*sem):
    return pltpu.CompilerParams(dimension_semantics=sem, vmem_limit_bytes=VMEM_LIMIT)


def _mm(a, b):
    return jnp.dot(a.astype(MXU_DTYPE), b.astype(MXU_DTYPE), preferred_element_type=F32)


def _mm_nt(a, b):
    return lax.dot_general(a.astype(MXU_DTYPE), b.astype(MXU_DTYPE), (((1,), (1,)), ((), ())),
                           preferred_element_type=F32)


def _mm_tn(a, b):
    return lax.dot_general(a.astype(MXU_DTYPE), b.astype(MXU_DTYPE), (((0,), (0,)), ((), ())),
                           preferred_element_type=F32)


def _silu(x):
    return x * jax.nn.sigmoid(x)


def _layer_norm_rows(r, g, b):
    mu = jnp.mean(r, axis=-1, keepdims=True)
    d = r - mu
    var = jnp.mean(d * d, axis=-1, keepdims=True)
    return d * lax.rsqrt(var + LN_EPS) * g + b


def _adaln_kernel(c_ref, w_ref, b_ref, o_ref):
    o_ref[...] = _mm(c_ref[...], w_ref[...]) + b_ref[...]


def _adaln(c, w, b):
    n_layers, d, n = w.shape
    rows = c.shape[0]
    tn = 1024
    return pl.pallas_call(
        _adaln_kernel,
        out_shape=jax.ShapeDtypeStruct((n_layers, rows, n), F32),
        grid=(n_layers, n // tn),
        in_specs=[pl.BlockSpec((rows, d), lambda l, j: (0, 0)),
                  pl.BlockSpec((None, d, tn), lambda l, j: (l, 0, j)),
                  pl.BlockSpec((None, 1, tn), lambda l, j: (l, 0, j))],
        out_specs=pl.BlockSpec((None, rows, tn), lambda l, j: (l, 0, j)),
        compiler_params=_cparams("arbitrary", "arbitrary"),
        name="adaln",
    )(c, w, b.reshape(n_layers, 1, n))


def _rope_proj_store(h_ref, w_ref, rc_ref, rs1_ref, rs2_ref, o_ref):
    half = ROPE_DIM // 2
    tm = h_ref.shape[0]
    rs = min(tm, ROW_SUBTILE)
    subs = [slice(s * rs, (s + 1) * rs) for s in range(tm // rs)]
    ys = [jnp.dot(h_ref[sl, :], w_ref[...], preferred_element_type=F32) for sl in subs]
    for sl, y in zip(subs, ys):
        c, s1, s2 = rc_ref[sl, :], rs1_ref[sl, :], rs2_ref[sl, :]
        for h in range(y.shape[1] // HEAD_DIM):
            yh = y[:, h * HEAD_DIM:(h + 1) * HEAD_DIM]
            o_ref[sl, h * HEAD_DIM:(h + 1) * HEAD_DIM] = (
                yh * c + pltpu.roll(yh, HEAD_DIM - half, 1) * s1 + pltpu.roll(yh, half, 1) * s2)


def _in_proj_kernel(x_ref, sc_ref, sh_ref, w_ref, wba_ref, rc_ref, rs1_ref, rs2_ref,
                    g_ref, ba_ref, mq_ref, mk_ref, mv_ref, h_ref, *, n_g, n_m):
    j = pl.program_id(1)

    @pl.when(j == 0)
    def _():
        h_ref[...] = (x_ref[...] * (1.0 + sc_ref[...]) + sh_ref[...]).astype(h_ref.dtype)
        ba_ref[...] = jnp.dot(h_ref[...], wba_ref[...], preferred_element_type=F32)

    def proj():
        return jnp.dot(h_ref[...], w_ref[...], preferred_element_type=F32)

    @pl.when(j < n_g)
    def _():
        g_ref[...] = proj()

    @pl.when((j >= n_g) & (j < n_g + n_m))
    def _():
        _rope_proj_store(h_ref, w_ref, rc_ref, rs1_ref, rs2_ref, mq_ref)

    @pl.when((j >= n_g + n_m) & (j < n_g + 2 * n_m))
    def _():
        _rope_proj_store(h_ref, w_ref, rc_ref, rs1_ref, rs2_ref, mk_ref)

    @pl.when(j >= n_g + 2 * n_m)
    def _():
        mv_ref[...] = proj()


def _in_proj(x, scale, shift, w_all, w_ba, rope, tm, tn, mod_index, rope_index):
    m, d = x.shape
    r = scale.shape[1]
    n_g, n_m = 4 * WIDTH // tn, WIDTH // tn
    mod_spec = pl.BlockSpec((None, r, d), lambda i, j: (mod_index(i), 0, 0))
    rope_spec = pl.BlockSpec((tm, LANES), lambda i, j: (rope_index(i), 0))

    def m_spec(first):
        return pl.BlockSpec((tm, tn), lambda i, j: (i, jnp.clip(j - first, 0, n_m - 1)))

    return pl.pallas_call(
        functools.partial(_in_proj_kernel, n_g=n_g, n_m=n_m),
        out_shape=(jax.ShapeDtypeStruct((m, 4 * WIDTH), F32),
                   jax.ShapeDtypeStruct((m, LANES), F32),
                   jax.ShapeDtypeStruct((m, WIDTH), F32),
                   jax.ShapeDtypeStruct((m, WIDTH), F32),
                   jax.ShapeDtypeStruct((m, WIDTH), F32)),
        grid=(m // tm, n_g + 3 * n_m),
        in_specs=[pl.BlockSpec((tm, d), lambda i, j: (i, 0)), mod_spec, mod_spec,
                  pl.BlockSpec((d, tn), lambda i, j: (0, j)),
                  pl.BlockSpec((d, LANES), lambda i, j: (0, 0)),
                  rope_spec, rope_spec, rope_spec],
        out_specs=(pl.BlockSpec((tm, tn), lambda i, j: (i, jnp.minimum(j, n_g - 1))),
                   pl.BlockSpec((tm, LANES), lambda i, j: (i, 0)),
                   m_spec(n_g), m_spec(n_g + n_m), m_spec(n_g + 2 * n_m)),
        scratch_shapes=[pltpu.VMEM((tm, d), MXU_DTYPE)],
        compiler_params=_cparams("arbitrary", "arbitrary"),
        name="in_proj",
    )(x, scale, shift, w_all, w_ba, *rope)


def _unit_lower_inverse(a_low, eye, same_blk):
    items = range(len(a_low))
    d = [jnp.where(same_blk, -a, 0.0) for a in a_low]
    o = [-a_low[i] - d[i] for i in items]
    x = [eye + d[i] for i in items]
    p = d
    for _ in range(int(math.log2(INV_BLOCK)) - 1):
        p = [_mm(p[i], p[i]) for i in items]
        x = [x[i] + _mm(x[i], p[i]) for i in items]
    y = [_mm(x[i], o[i]) for i in items]
    z = [eye + y[i] for i in items]
    p = y
    for _ in range(int(math.log2(GDN_CHUNK // INV_BLOCK)) - 1):
        p = [_mm(p[i], p[i]) for i in items]
        z = [z[i] + _mm(z[i], p[i]) for i in items]
    return z, x


def _gdn_wy(q, k, v, beta_c, eg_c, gc_col, gc_row, masks):
    eye, same_blk, tri_incl, tri_strict = masks
    items = range(len(q))
    c = q[0].shape[0]
    kb = [k[i] * beta_c[i] for i in items]
    p = [_mm_nt(jnp.concatenate([q[i], kb[i]], axis=0), k[i]) for i in items]
    decay = [jnp.exp(jnp.where(tri_incl, gc_col[i] - gc_row[i], NEG_INF)) for i in items]
    qk = [p[i][:c] * decay[i] for i in items]
    a_low = [jnp.where(tri_strict, p[i][c:] * decay[i], 0.0) for i in items]
    z, x = _unit_lower_inverse(a_low, eye, same_blk)
    rhs = [jnp.concatenate([v[i] * beta_c[i], kb[i] * eg_c[i]], axis=1) for i in items]
    t = [_mm(x[i], rhs[i]) for i in items]
    sol = [_mm(z[i], t[i]) for i in items]
    return [s[:, :HEAD_DIM] for s in sol], [s[:, HEAD_DIM:] for s in sol], qk


def _gdn_state_step(q, k, u, w, qk, eg_c, ek_c, gl, s):
    items = range(len(q))
    v_new = [u[i] - _mm(w[i], s[i]) for i in items]
    o = [_mm(q[i] * eg_c[i], s[i]) + _mm(qk[i], v_new[i]) for i in items]
    s_new = [s[i] * gl[i] + _mm_tn(k[i] * ek_c[i], v_new[i]) for i in items]
    return o, s_new


def _gdn_kernel(x_ref, ba_ref, cw_ref, alog_ref, dtb_ref, ng_ref, *rest, tt, n_valid, from_state):
    if from_state:
        cs_ref, s0_ref, o_ref, sout_ref, xbuf, zbuf, babuf, s_scr = rest
    else:
        o_ref, sout_ref, xbuf, s_scr = rest
    t = pl.program_id(1)
    nk = GDN_CONV - 1
    c = GDN_CHUNK

    if from_state:
        xbuf[...] = jnp.zeros(xbuf.shape, F32)
        xbuf[SUBLANES - nk:SUBLANES, :] = cs_ref[...]
        xbuf[SUBLANES:SUBLANES + n_valid, :] = x_ref[:, 0:3 * WIDTH]
        zbuf[...] = jnp.zeros(zbuf.shape, F32)
        zbuf[0:n_valid, :] = x_ref[:, 3 * WIDTH:4 * WIDTH]
        babuf[...] = jnp.zeros(babuf.shape, F32)
        babuf[0:n_valid, :] = ba_ref[...]
        s_scr[...] = s0_ref[...]
        ba = babuf[...]
    else:
        @pl.when(t == 0)
        def _():
            xbuf[0:SUBLANES, :] = jnp.zeros((SUBLANES, 3 * WIDTH), F32)
            s_scr[...] = jnp.zeros(s_scr.shape, F32)

        @pl.when(t > 0)
        def _():
            xbuf[0:SUBLANES, :] = xbuf[tt:tt + SUBLANES, :]

        xbuf[SUBLANES:SUBLANES + tt, :] = x_ref[:, 0:3 * WIDTH]
        ba = ba_ref[...]

    beta_all = jax.nn.sigmoid(ba)
    a_in = ba + dtb_ref[...]
    softplus = jnp.maximum(a_in, 0.0) + jnp.log1p(jnp.exp(-jnp.abs(a_in)))
    g_all = -jnp.exp(alog_ref[...]) * softplus
    if n_valid < tt:
        live = lax.broadcasted_iota(jnp.int32, (tt, LANES), 0) < n_valid
        beta_all = jnp.where(live, beta_all, 0.0)
        g_all = jnp.where(live, g_all, 0.0)

    row = lax.broadcasted_iota(jnp.int32, (c, c), 0)
    col = lax.broadcasted_iota(jnp.int32, (c, c), 1)
    tri_incl = row >= col
    tri_strict = row > col
    masks = (jnp.where(row == col, 1.0, 0.0).astype(F32), (row // INV_BLOCK) == (col // INV_BLOCK),
             tri_incl, tri_strict)
    ltri = jnp.where(tri_incl, 1.0, 0.0).astype(F32)

    def conv_act(col0):
        acc = xbuf[SUBLANES - nk:SUBLANES - nk + tt, col0:col0 + HEAD_DIM] * cw_ref[0:1, col0:col0 + HEAD_DIM]
        for i in range(1, GDN_CONV):
            acc = acc + (xbuf[SUBLANES - nk + i:SUBLANES - nk + i + tt, col0:col0 + HEAD_DIM]
                         * cw_ref[i:i + 1, col0:col0 + HEAD_DIM])
        return _silu(acc)

    def l2n(a):
        return a * lax.rsqrt(jnp.sum(a * a, axis=-1, keepdims=True) + NORM_EPS)

    chunk_decay = []
    for ci in range(tt // c):
        gcum = jnp.dot(ltri, g_all[ci * c:(ci + 1) * c, :], precision=lax.Precision.HIGHEST,
                       preferred_element_type=F32)
        chunk_decay.append((gcum, gcum.T, jnp.exp(gcum), jnp.exp(gcum[c - 1:c, :] - gcum)))

    heads = range(N_HEADS)
    n_chunks = tt // c
    q_t = [l2n(conv_act(h * HEAD_DIM)) * (HEAD_DIM ** -0.5) for h in heads]
    k_t = [l2n(conv_act(WIDTH + h * HEAD_DIM)) for h in heads]
    v_t = [conv_act(2 * WIDTH + h * HEAD_DIM) for h in heads]

    pairs = [(ci, h) for ci in range(n_chunks) for h in heads]
    rows = lambda a, ci: a[ci * c:(ci + 1) * c]
    lane = lambda a, h: a[:, N_HEADS + h:N_HEADS + h + 1]
    qs = [rows(q_t[h], ci) for ci, h in pairs]
    ks = [rows(k_t[h], ci) for ci, h in pairs]
    egs = [lane(chunk_decay[ci][2], h) for ci, h in pairs]
    u, w, qk = _gdn_wy(
        qs, ks, [rows(v_t[h], ci) for ci, h in pairs],
        [rows(beta_all, ci)[:, h:h + 1] for ci, h in pairs], egs,
        [lane(chunk_decay[ci][0], h) for ci, h in pairs],
        [chunk_decay[ci][1][N_HEADS + h:N_HEADS + h + 1, :] for ci, h in pairs], masks)

    s = [s_scr[h] for h in heads]
    for ci in range(n_chunks):
        sl = slice(ci * N_HEADS, (ci + 1) * N_HEADS)
        eks = [lane(chunk_decay[ci][3], h) for h in heads]
        gls = [lane(chunk_decay[ci][2], h)[c - 1:c] for h in heads]
        o, s = _gdn_state_step(qs[sl], ks[sl], u[sl], w[sl], qk[sl], egs[sl], eks, gls, s)
        r0 = ci * c
        for h in heads:
            hs = slice(h * HEAD_DIM, (h + 1) * HEAD_DIM)
            if from_state:
                zh = zbuf[r0:r0 + c, hs]
            else:
                zh = x_ref[r0:r0 + c, 3 * WIDTH + h * HEAD_DIM:3 * WIDTH + (h + 1) * HEAD_DIM]
            on = o[h] * lax.rsqrt(jnp.mean(o[h] * o[h], axis=-1, keepdims=True) + NORM_EPS) * ng_ref[...] * _silu(zh)
            if from_state:
                o_ref[:, hs] = on[0:n_valid].astype(o_ref.dtype)
            else:
                o_ref[r0:r0 + c, hs] = on.astype(o_ref.dtype)
    for h in heads:
        s_scr[h] = s[h]
        sout_ref[h] = s[h]


def _gdn_prompt(g, ba, conv_w, alog_pad, dtb_pad, norm_g, batch, seq, tt):
    m = g.shape[0]
    nt = seq // tt
    kern = functools.partial(_gdn_kernel, tt=tt, n_valid=tt, from_state=False)
    small = lambda shape: pl.BlockSpec(shape, lambda b, t: (0, 0))
    return pl.pallas_call(
        kern,
        out_shape=(jax.ShapeDtypeStruct((m, WIDTH), MXU_DTYPE),
                   jax.ShapeDtypeStruct((batch, N_HEADS, HEAD_DIM, HEAD_DIM), F32)),
        grid=(batch, nt),
        in_specs=[pl.BlockSpec((tt, 4 * WIDTH), lambda b, t: (b * nt + t, 0)),
                  pl.BlockSpec((tt, LANES), lambda b, t: (b * nt + t, 0)),
                  small((GDN_CONV, 3 * WIDTH)), small((1, LANES)), small((1, LANES)), small((1, HEAD_DIM))],
        out_specs=(pl.BlockSpec((tt, WIDTH), lambda b, t: (b * nt + t, 0)),
                   pl.BlockSpec((None, N_HEADS, HEAD_DIM, HEAD_DIM), lambda b, t: (b, 0, 0, 0))),
        scratch_shapes=[pltpu.VMEM((tt + SUBLANES, 3 * WIDTH), F32),
                        pltpu.VMEM((N_HEADS, HEAD_DIM, HEAD_DIM), F32)],
        compiler_params=_cparams("arbitrary", "arbitrary"),
        name="gdn_prompt",
    )(g, ba, conv_w, alog_pad, dtb_pad, norm_g)


def _gdn_sample(g, ba, conv_state, s0, conv_w, alog_pad, dtb_pad, norm_g, batch, dec_seq):
    tt = GDN_CHUNK
    kern = functools.partial(_gdn_kernel, tt=tt, n_valid=dec_seq, from_state=True)
    small = lambda shape: pl.BlockSpec(shape, lambda b, t: (0, 0))
    return pl.pallas_call(
        kern,
        out_shape=(jax.ShapeDtypeStruct((batch, dec_seq, WIDTH), F32),
                   jax.ShapeDtypeStruct((batch, N_HEADS, HEAD_DIM, HEAD_DIM), F32)),
        grid=(batch, 1),
        in_specs=[pl.BlockSpec((None, dec_seq, 4 * WIDTH), lambda b, t: (b, 0, 0)),
                  pl.BlockSpec((None, dec_seq, LANES), lambda b, t: (b, 0, 0)),
                  small((GDN_CONV, 3 * WIDTH)), small((1, LANES)), small((1, LANES)), small((1, HEAD_DIM)),
                  pl.BlockSpec((None, GDN_CONV - 1, 3 * WIDTH), lambda b, t: (b, 0, 0)),
                  pl.BlockSpec((None, N_HEADS, HEAD_DIM, HEAD_DIM), lambda b, t: (b, 0, 0, 0))],
        out_specs=(pl.BlockSpec((None, dec_seq, WIDTH), lambda b, t: (b, 0, 0)),
                   pl.BlockSpec((None, N_HEADS, HEAD_DIM, HEAD_DIM), lambda b, t: (b, 0, 0, 0))),
        scratch_shapes=[pltpu.VMEM((tt + SUBLANES, 3 * WIDTH), F32),
                        pltpu.VMEM((tt, WIDTH), F32),
                        pltpu.VMEM((tt, LANES), F32),
                        pltpu.VMEM((N_HEADS, HEAD_DIM, HEAD_DIM), F32)],
        compiler_params=_cparams("arbitrary", "arbitrary"),
        name="gdn_sample",
    )(g, ba, conv_w, alog_pad, dtb_pad, norm_g, conv_state, s0)


def _topk_select(gate, n_valid, idx, axis):
    n = gate.shape[axis]
    cnt = jnp.zeros(gate.shape, F32)
    for j in range(n):
        cand = lax.slice_in_dim(gate, j, j + 1, axis=axis)
        beats = jnp.where(cand > gate, 1.0, jnp.where((cand == gate) & (j < idx), 1.0, 0.0))
        cnt = cnt + jnp.where(j < n_valid, beats, 0.0)
    return (cnt < MOBA_TOPK) & (idx < n_valid)


MOBA_HEADS_PER_STEP = 4


def _moba_prompt_kernel(q_ref, k_ref, v_ref, o_ref, k16, vt16, kmean, *, nb):
    i = pl.program_id(2)
    blk = MOBA_BLOCK
    scale = HEAD_DIM ** -0.5
    group = range(MOBA_HEADS_PER_STEP)
    lanes = lambda g: slice(g * HEAD_DIM, (g + 1) * HEAD_DIM)

    @pl.when(i == 0)
    def _():
        for g in group:
            for j in range(nb):
                kj = k_ref[j * blk:(j + 1) * blk, lanes(g)]
                k16[g, j] = kj.astype(MXU_DTYPE)
                kmean[g, j:j + 1, :] = jnp.mean(kj, axis=0, keepdims=True)
                vt16[g, j] = v_ref[j * blk:(j + 1) * blk, lanes(g)].T.astype(MXU_DTYPE)

    kpos = lax.broadcasted_iota(jnp.int32, (blk, blk), 0)
    qpos = lax.broadcasted_iota(jnp.int32, (blk, blk), 1)

    def attend(cur):
        qt = [q_ref[:, lanes(g)].T for g in group]
        qt16 = [a.astype(MXU_DTYPE) for a in qt]
        if cur > MOBA_TOPK:
            gate = [jnp.dot(kmean[g], qt[g], precision=lax.Precision.HIGHEST, preferred_element_type=F32)
                    for g in group]
            blk_idx = lax.broadcasted_iota(jnp.int32, gate[0].shape, 0)
            sel = [jnp.where(_topk_select(gate[g], cur, blk_idx, 0), 1.0, 0.0) for g in group]
        s = [[] for _ in group]
        for j in range(cur + 1):
            for g in group:
                sj = jnp.dot(k16[g, j], qt16[g], preferred_element_type=F32) * scale
                if j == cur:
                    sj = jnp.where(kpos <= qpos, sj, NEG_INF)
                elif cur > MOBA_TOPK:
                    sj = jnp.where(sel[g][j:j + 1, :] > 0.5, sj, NEG_INF)
                s[g].append(sj)
        m = [functools.reduce(jnp.maximum, [jnp.max(sj, axis=0, keepdims=True) for sj in s[g]]) for g in group]
        l = [jnp.zeros((1, blk), F32) for _ in group]
        acc = [jnp.zeros((HEAD_DIM, blk), F32) for _ in group]
        for j in range(cur + 1):
            for g in group:
                p = jnp.exp(s[g][j] - m[g])
                l[g] = l[g] + jnp.sum(p, axis=0, keepdims=True)
                acc[g] = acc[g] + jnp.dot(vt16[g, j], p.astype(MXU_DTYPE), preferred_element_type=F32)
        for g in group:
            o_ref[:, lanes(g)] = (acc[g] / l[g]).T.astype(o_ref.dtype)

    for cur in range(nb):
        pl.when(i == cur)(functools.partial(attend, cur))


def _moba_prompt(mq, mk, mv, batch, seq):
    m = mq.shape[0]
    nb = seq // MOBA_BLOCK
    hps = MOBA_HEADS_PER_STEP
    kern = functools.partial(_moba_prompt_kernel, nb=nb)
    kv_spec = pl.BlockSpec((seq, hps * HEAD_DIM), lambda b, h, i: (b, h))
    q_spec = pl.BlockSpec((MOBA_BLOCK, hps * HEAD_DIM), lambda b, h, i: (b * nb + i, h))
    return pl.pallas_call(
        kern,
        out_shape=jax.ShapeDtypeStruct((m, WIDTH), MXU_DTYPE),
        grid=(batch, N_HEADS // hps, nb),
        in_specs=[q_spec, kv_spec, kv_spec],
        out_specs=q_spec,
        scratch_shapes=[pltpu.VMEM((hps, nb, MOBA_BLOCK, HEAD_DIM), MXU_DTYPE),
                        pltpu.VMEM((hps, nb, HEAD_DIM, MOBA_BLOCK), MXU_DTYPE),
                        pltpu.VMEM((hps, nb, HEAD_DIM), F32)],
        compiler_params=_cparams("arbitrary", "arbitrary", "arbitrary"),
        name="moba_prompt",
    )(mq, mk, mv)


QPAD = SUBLANES


PAST_BLOCKS_PER_STEP = 8


def _moba_partial_kernel(pt_ref, q_ref, *refs):
    nblk, ppb = PAST_BLOCKS_PER_STEP, MOBA_BLOCK // PAGE_SIZE
    k_refs, v_refs = refs[:nblk * ppb], refs[nblk * ppb:2 * nblk * ppb]
    kmean_ref, pm_ref, pl_ref, pacc_ref = refs[2 * nblk * ppb:]
    n = pl.program_id(1)
    scale = HEAD_DIM ** -0.5
    rows = PAGE_SIZE * N_HEADS
    q16 = q_ref[...].astype(MXU_DTYPE)
    q_head = lax.broadcasted_iota(jnp.int32, (N_HEADS * QPAD, rows), 0) // QPAD
    k_head = lax.broadcasted_iota(jnp.int32, (N_HEADS * QPAD, rows), 1) % N_HEADS
    same_head = q_head == k_head
    for g in range(nblk):
        ks = [k_refs[g * ppb + pg][...] for pg in range(ppb)]
        kmean_ref[n * nblk + g] = sum(jnp.sum(k, axis=0) for k in ks) * (1.0 / MOBA_BLOCK)
        s = [jnp.where(same_head, _mm_nt(q16, k.reshape(rows, HEAD_DIM)) * scale, NEG_INF) for k in ks]
        mx = functools.reduce(jnp.maximum, [jnp.max(sp, axis=-1, keepdims=True) for sp in s])
        p = [jnp.exp(sp - mx) for sp in s]
        lsum = sum(jnp.sum(pp, axis=-1, keepdims=True) for pp in p)
        pm_ref[g] = jnp.broadcast_to(mx, pm_ref.shape[1:])
        pl_ref[g] = jnp.broadcast_to(lsum, pl_ref.shape[1:])
        pacc_ref[g] = sum(_mm(p[pg], v_refs[g * ppb + pg][...].reshape(rows, HEAD_DIM)) for pg in range(ppb))


def _moba_partials(page_table_flat, q, cache_k, cache_v, layer, batch, nb, n_pages):
    nblk, ppb = PAST_BLOCKS_PER_STEP, MOBA_BLOCK // PAGE_SIZE
    assert nb % nblk == 0
    hq = N_HEADS * QPAD

    def page_spec(g, pg):
        return pl.BlockSpec((None, None, PAGE_SIZE, N_HEADS, HEAD_DIM),
                            lambda b, n, pt: (layer, pt[b * n_pages + ppb * (n * nblk + g) + pg], 0, 0, 0))

    page_specs = [page_spec(g, pg) for g in range(nblk) for pg in range(ppb)]
    part_shape = jax.ShapeDtypeStruct((batch, nb, hq, LANES), F32)
    part_spec = pl.BlockSpec((None, nblk, hq, LANES), lambda b, n, pt: (b, n, 0, 0))
    return pl.pallas_call(
        _moba_partial_kernel,
        out_shape=(jax.ShapeDtypeStruct((batch, nb, N_HEADS, HEAD_DIM), F32), part_shape, part_shape, part_shape),
        grid_spec=pltpu.PrefetchScalarGridSpec(
            num_scalar_prefetch=1,
            grid=(batch, nb // nblk),
            in_specs=[pl.BlockSpec((None, hq, HEAD_DIM), lambda b, n, pt: (b, 0, 0))] + page_specs + page_specs,
            out_specs=(pl.BlockSpec((None, nb, N_HEADS, HEAD_DIM), lambda b, n, pt: (b, 0, 0, 0)),
                       part_spec, part_spec, part_spec)),
        compiler_params=_cparams("arbitrary", "arbitrary"),
        name="moba_sample_partials",
    )(page_table_flat, q, *([cache_k] * (nblk * ppb)), *([cache_v] * (nblk * ppb)))


def _moba_merge_kernel(q_ref, k_ref, v_ref, kmean_ref, pm_ref, pl_ref, pacc_ref, o_ref, *, dec_seq, nb):
    scale = HEAD_DIM ** -0.5
    q8, k8, v8 = q_ref[...], k_ref[...], v_ref[...]
    qi = lax.broadcasted_iota(jnp.int32, (QPAD, QPAD), 0)
    kj = lax.broadcasted_iota(jnp.int32, (QPAD, QPAD), 1)
    own_ok = (kj <= qi) & (kj < dec_seq)
    blk_idx = lax.broadcasted_iota(jnp.int32, (QPAD, nb), 1)
    for h in range(N_HEADS):
        hs = slice(h * HEAD_DIM, (h + 1) * HEAD_DIM)
        hq = slice(h * QPAD, (h + 1) * QPAD)
        gate = lax.dot_general(q8[:, hs], kmean_ref[:, h, :], (((1,), (1,)), ((), ())),
                               precision=lax.Precision.HIGHEST, preferred_element_type=F32)
        sel = jnp.where(_topk_select(gate, nb, blk_idx, 1), 1.0, 0.0)
        s_own = jnp.where(own_ok, _mm_nt(q8[:, hs], k8[:, hs]) * scale, NEG_INF)
        m_tot = jnp.broadcast_to(jnp.max(s_own, axis=-1, keepdims=True), (QPAD, LANES))
        for n in range(nb):
            m_tot = jnp.maximum(m_tot, jnp.where(sel[:, n:n + 1] > 0.5, pm_ref[n, hq, :], NEG_INF))
        p_own = jnp.exp(s_own - m_tot[:, 0:1])
        l_tot = jnp.broadcast_to(jnp.sum(p_own, axis=-1, keepdims=True), (QPAD, LANES))
        acc = jnp.zeros((QPAD, HEAD_DIM), F32)
        for j in range(dec_seq):
            acc = acc + p_own[:, j:j + 1] * v8[j:j + 1, hs]
        for n in range(nb):
            w = jnp.where(sel[:, n:n + 1] > 0.5, jnp.exp(pm_ref[n, hq, :] - m_tot), 0.0)
            l_tot = l_tot + w * pl_ref[n, hq, :]
            acc = acc + w * pacc_ref[n, hq, :]
        o_ref[:, hs] = (acc / l_tot)[0:dec_seq].astype(o_ref.dtype)


def _moba_merge(q, k, v, kmean, pm, plsum, pacc, batch, dec_seq, nb):
    kern = functools.partial(_moba_merge_kernel, dec_seq=dec_seq, nb=nb)
    row_spec = pl.BlockSpec((None, QPAD, WIDTH), lambda b: (b, 0, 0))
    part_spec = pl.BlockSpec((None, nb, N_HEADS * QPAD, LANES), lambda b: (b, 0, 0, 0))
    return pl.pallas_call(
        kern,
        out_shape=jax.ShapeDtypeStruct((batch, dec_seq, WIDTH), F32),
        grid=(batch,),
        in_specs=[row_spec, row_spec, row_spec,
                  pl.BlockSpec((None, nb, N_HEADS, HEAD_DIM), lambda b: (b, 0, 0, 0)),
                  part_spec, part_spec, part_spec],
        out_specs=pl.BlockSpec((None, dec_seq, WIDTH), lambda b: (b, 0, 0)),
        compiler_params=_cparams("arbitrary"),
        name="moba_sample_merge",
    )(q, k, v, kmean, pm, plsum, pacc)


def _out_ln_kernel(og_ref, om_ref, x_ref, gate_ref, w1_ref, w2_ref, lng_ref, lnb_ref, o_ref, *, alpha):
    tm = x_ref.shape[0]
    rs = min(tm, ROW_SUBTILE)
    subs = [slice(s * rs, (s + 1) * rs) for s in range(tm // rs)]
    y = [_mm(og_ref[sl, :], w1_ref[...]) + _mm(om_ref[sl, :], w2_ref[...]) for sl in subs]
    for sl, ys in zip(subs, y):
        gate = gate_ref[...] if gate_ref.shape[0] == 1 else gate_ref[sl, :]
        r = alpha * x_ref[sl, :] + (1.0 + gate) * ys
        o_ref[sl, :] = _layer_norm_rows(r, lng_ref[...], lnb_ref[...])


def _out_ln(o_gdn, o_moba, x, gate, w_out, ln_g, ln_b, tm, mod_index, alpha):
    m, d = x.shape
    r = gate.shape[1]
    vec = pl.BlockSpec((1, d), lambda i: (0, 0))
    return pl.pallas_call(
        functools.partial(_out_ln_kernel, alpha=alpha),
        out_shape=jax.ShapeDtypeStruct((m, d), F32),
        grid=(m // tm,),
        in_specs=[pl.BlockSpec((tm, WIDTH), lambda i: (i, 0)), pl.BlockSpec((tm, WIDTH), lambda i: (i, 0)),
                  pl.BlockSpec((tm, d), lambda i: (i, 0)),
                  pl.BlockSpec((None, r, d), lambda i: (mod_index(i), 0, 0)),
                  pl.BlockSpec((WIDTH, d), lambda i: (0, 0)), pl.BlockSpec((WIDTH, d), lambda i: (1, 0)),
                  vec, vec],
        out_specs=pl.BlockSpec((tm, d), lambda i: (i, 0)),
        compiler_params=_cparams("arbitrary"),
        name="out_proj_ln",
    )(o_gdn, o_moba, x, gate, w_out, w_out, ln_g, ln_b)


def _ffn_kernel(x_ref, sc_ref, sh_ref, gate_ref, wg_ref, wu_ref, wd_ref, cw_ref, cb_ref, lng_ref, lnb_ref, *rest,
                tm, n_sub, n_chunks, tiles_per_seq, dec_seq, alpha):
    sample = dec_seq is not None
    if sample:
        p1_ref, p2_ref, o_ref, st_ref, h_ref, acc_ref, gbuf = rest
    else:
        o_ref, st_ref, h_ref, acc_ref, gbuf, carry = rest
    i = pl.program_id(0)
    c = pl.program_id(1)
    tc = wg_ref.shape[1]
    rs = tm // n_sub
    subs = [slice(s * rs, (s + 1) * rs) for s in range(n_sub)]
    mod_rows = lambda ref, sl: ref[...] if ref.shape[0] == 1 else ref[sl, :]

    def chunk_step(first_chunk, last_chunk):
        if sample:
            gbuf[0:SUBLANES, :] = jnp.zeros((SUBLANES, tc), F32)
        else:
            seq_start = (i % tiles_per_seq) == 0

            @pl.when(seq_start)
            def _():
                gbuf[0:SUBLANES, :] = jnp.zeros((SUBLANES, tc), F32)

            @pl.when(jnp.logical_not(seq_start))
            def _():
                gbuf[0:SUBLANES, :] = carry[c]

        g, u = [], []
        for s, sl in enumerate(subs):
            if first_chunk:
                h_ref[sl, :] = (x_ref[sl, :] * (1.0 + mod_rows(sc_ref, sl)) + mod_rows(sh_ref, sl)).astype(h_ref.dtype)
            g.append(jnp.dot(h_ref[sl, :], wg_ref[...], preferred_element_type=F32))
            gbuf[SUBLANES + s * rs:SUBLANES + (s + 1) * rs, :] = g[s]
            u.append(jnp.dot(h_ref[sl, :], wu_ref[...], preferred_element_type=F32))
        if not sample:
            carry[c] = g[-1][rs - SUBLANES:rs, :]
            st_ref[...] = g[-1][rs - (FFN_CONV - 1):rs, :]
        for s, sl in enumerate(subs):
            prev1 = gbuf[SUBLANES - 1 + s * rs:SUBLANES - 1 + (s + 1) * rs, :]
            prev2 = gbuf[SUBLANES - 2 + s * rs:SUBLANES - 2 + (s + 1) * rs, :]
            if sample:
                pos = lax.broadcasted_iota(jnp.int32, (rs, tc), 0) % dec_seq
                prev1 = jnp.where(pos >= 1, prev1, p1_ref[sl, :])
                prev2 = jnp.where(pos >= 2, prev2, p2_ref[sl, :])
                st_ref[sl, :] = g[s]
            conv = prev2 * cw_ref[0:1, :] + prev1 * cw_ref[1:2, :] + g[s] * cw_ref[2:3, :]
            act = _silu(conv + cb_ref[...]) * u[s]
            down = jnp.dot(act.astype(MXU_DTYPE), wd_ref[...], preferred_element_type=F32)
            total = down if first_chunk else acc_ref[sl, :] + down
            if last_chunk:
                r = alpha * x_ref[sl, :] + (1.0 + mod_rows(gate_ref, sl)) * total
                o_ref[sl, :] = _layer_norm_rows(r, lng_ref[...], lnb_ref[...])
            else:
                acc_ref[sl, :] = total

    assert n_chunks >= 2
    pl.when(c == 0)(functools.partial(chunk_step, True, False))
    pl.when((c > 0) & (c < n_chunks - 1))(functools.partial(chunk_step, False, False))
    pl.when(c == n_chunks - 1)(functools.partial(chunk_step, False, True))


def _ffn(x, scale, shift, gate, w_up, w_down, conv_w, conv_b, ln_g, ln_b, tm, tc, mod_index, alpha,
         tiles_per_seq=None, prev_rows=None, dec_seq=None):
    m, d = x.shape
    d_ff = w_down.shape[0]
    n_chunks = d_ff // tc
    r = scale.shape[1]
    sample = prev_rows is not None
    mod_spec = pl.BlockSpec((None, r, d), lambda i, c: (mod_index(i), 0, 0))
    vec = pl.BlockSpec((1, d), lambda i, c: (0, 0))
    in_specs = [pl.BlockSpec((tm, d), lambda i, c: (i, 0)), mod_spec, mod_spec, mod_spec,
                pl.BlockSpec((d, tc), lambda i, c: (0, c)),
                pl.BlockSpec((d, tc), lambda i, c: (0, n_chunks + c)),
                pl.BlockSpec((tc, d), lambda i, c: (c, 0)),
                pl.BlockSpec((FFN_CONV, tc), lambda i, c: (0, c)),
                pl.BlockSpec((1, tc), lambda i, c: (0, c)),
                vec, vec]
    args = [x, scale, shift, gate, w_up, w_up, w_down, conv_w, conv_b, ln_g, ln_b]
    scratch = [pltpu.VMEM((tm, d), MXU_DTYPE), pltpu.VMEM((tm, d), F32), pltpu.VMEM((tm + SUBLANES, tc), F32)]
    if sample:
        in_specs += [pl.BlockSpec((tm, tc), lambda i, c: (i, c))] * 2
        args += list(prev_rows)
        st_shape = jax.ShapeDtypeStruct((m, d_ff), F32)
        st_spec = pl.BlockSpec((tm, tc), lambda i, c: (i, c))
    else:
        st_shape = jax.ShapeDtypeStruct((m // tm, FFN_CONV - 1, d_ff), F32)
        st_spec = pl.BlockSpec((None, FFN_CONV - 1, tc), lambda i, c: (i, 0, c))
        scratch.append(pltpu.VMEM((n_chunks, SUBLANES, tc), F32))
    kern = functools.partial(_ffn_kernel, tm=tm, n_sub=max(1, tm // ROW_SUBTILE), n_chunks=n_chunks,
                             tiles_per_seq=tiles_per_seq,
                             dec_seq=dec_seq if sample else None, alpha=alpha)
    return pl.pallas_call(
        kern,
        out_shape=(jax.ShapeDtypeStruct((m, d), F32), st_shape),
        grid=(m // tm, n_chunks),
        in_specs=in_specs,
        out_specs=(pl.BlockSpec((tm, d), lambda i, c: (i, 0)), st_spec),
        scratch_shapes=scratch,
        compiler_params=_cparams("arbitrary", "arbitrary"),
        name="ffn_sample" if sample else "ffn_prompt",
    )(*args)


def _rope_tables(pos):
    half = ROPE_DIM // 2
    inv_freq = ROPE_THETA ** (-jnp.arange(half, dtype=F32) / half)
    ang = pos.astype(F32)[:, None] * inv_freq[None, :]
    cos, sin = jnp.cos(ang), jnp.sin(ang)
    n = pos.shape[0]
    ones = jnp.ones((n, HEAD_DIM - ROPE_DIM), F32)
    zeros = jnp.zeros((n, HEAD_DIM - ROPE_DIM), F32)
    zh = jnp.zeros((n, half), F32)
    return (jnp.concatenate([cos, cos, ones], axis=1),
            jnp.concatenate([-sin, zh, zeros], axis=1),
            jnp.concatenate([zh, sin, zeros], axis=1))


def _forward(x_prompt, x_sample, cache_k, cache_v, state_gdn, state_gdn_conv, state_ffn_conv, page_table,
             c_prompt, c_sample, w_ada_mix, b_ada_mix, w_in, gdn_conv_w, gdn_a_log, gdn_dt_bias, gdn_norm_g,
             w_out, ln_mix_g, ln_mix_b, w_ada_ffn, b_ada_ffn, w_up, ffn_conv_w, ffn_conv_b, w_down,
             ln_ffn_g, ln_ffn_b, *, past_len, tm_prompt, tm_in, tn_in, tc_ffn, tt_gdn):
    bsz, seq, d = x_prompt.shape
    dbsz, dseq, _ = x_sample.shape
    depth = w_in.shape[0]
    d_ff = w_down.shape[1]
    alpha = (2 * depth) ** 0.25
    n_pool = cache_k.shape[1]
    n_pages = page_table.shape[1]
    nb_past = past_len // MOBA_BLOCK
    assert d == 2 * WIDTH and past_len % MOBA_BLOCK == 0 and nb_past >= MOBA_TOPK
    assert n_pages * PAGE_SIZE == past_len and seq % MOBA_BLOCK == 0
    mp, ms = bsz * seq, dbsz * dseq
    tps = seq // tm_prompt

    gw, nh = WIDTH, N_HEADS

    w_main = jnp.concatenate([w_in[:, :, :4 * gw], w_in[:, :, 4 * gw + 2 * nh:]], axis=2).astype(MXU_DTYPE)
    w_ba = jnp.pad(w_in[:, :, 4 * gw:4 * gw + 2 * nh], ((0, 0), (0, 0), (0, LANES - 2 * nh))).astype(MXU_DTYPE)
    w_out16, w_up16, w_down16 = (w.astype(MXU_DTYPE) for w in (w_out, w_up, w_down))
    alog_pad = jnp.pad(gdn_a_log, ((0, 0), (nh, LANES - 2 * nh)))[:, None, :]
    dtb_pad = jnp.pad(gdn_dt_bias, ((0, 0), (nh, LANES - 2 * nh)))[:, None, :]

    c_all = jnp.concatenate([c_prompt, c_sample], axis=0)
    mod_mix = _adaln(c_all, w_ada_mix, b_ada_mix)
    mod_ffn = _adaln(c_all, w_ada_ffn, b_ada_ffn)

    def mods(mod, l):
        out_p, out_s = [], []
        for part in range(3):
            v = mod[l, :, part * d:(part + 1) * d]
            out_p.append(v[:bsz, None, :])
            out_s.append(jnp.repeat(v[bsz:], dseq, axis=0)[None])
        return out_p, out_s

    rope_p = _rope_tables(jnp.arange(seq))
    rope_s = _rope_tables(jnp.tile(past_len + jnp.arange(dseq), dbsz))
    pt_flat = page_table.reshape(-1)

    xp = x_prompt.reshape(mp, d)
    xs = x_sample.reshape(ms, d)
    p_idx = lambda i: i // tps
    zero_idx = lambda i: 0
    outs = [[] for _ in range(10)]
    for l in range(depth):
        (shift_p, scale_p, gate_p), (shift_s, scale_s, gate_s) = mods(mod_mix, l)
        (fshift_p, fscale_p, fgate_p), (fshift_s, fscale_s, fgate_s) = mods(mod_ffn, l)
        lng, lnb = ln_mix_g[l][None], ln_mix_b[l][None]
        flng, flnb = ln_ffn_g[l][None], ln_ffn_b[l][None]
        ng = gdn_norm_g[l][None]

        tps_in = seq // tm_in
        g, ba, mq, mk, mv = _in_proj(xp, scale_p, shift_p, w_main[l], w_ba[l], rope_p, tm_in, tn_in,
                                     lambda i: i // tps_in, lambda i: i % tps_in)
        o_gdn, s_p = _gdn_prompt(g, ba, gdn_conv_w[l], alog_pad[l], dtb_pad[l], ng, bsz, seq, tt_gdn)
        o_moba = _moba_prompt(mq, mk, mv, bsz, seq)
        xp = _out_ln(o_gdn, o_moba, xp, gate_p, w_out16[l], lng, lnb, tm_prompt, p_idx, alpha)
        xp, fcp = _ffn(xp, fscale_p, fshift_p, fgate_p, w_up16[l], w_down16[l], ffn_conv_w[l], ffn_conv_b[l][None],
                       flng, flnb, tm_prompt, tc_ffn, p_idx, alpha, tiles_per_seq=tps)
        fcp = fcp[tps - 1::tps]
        gcp = g.reshape(bsz, seq, 4 * WIDTH)[:, seq - (GDN_CONV - 1):, :3 * WIDTH]

        gs, bas, mqs, mks, mvs = _in_proj(xs, scale_s, shift_s, w_main[l], w_ba[l], rope_s, ms, WIDTH,
                                          zero_idx, zero_idx)
        gs3 = gs.reshape(dbsz, dseq, 4 * WIDTH)
        o_gdn_s, s_s = _gdn_sample(gs3, bas.reshape(dbsz, dseq, LANES), state_gdn_conv[l], state_gdn[l],
                                   gdn_conv_w[l], alog_pad[l], dtb_pad[l], ng, dbsz, dseq)
        mq3, mk3, mv3 = (a.reshape(dbsz, dseq, WIDTH) for a in (mqs, mks, mvs))
        mq8, mk8, mv8 = (jnp.pad(a, ((0, 0), (0, QPAD - dseq), (0, 0))) for a in (mq3, mk3, mv3))
        mq_hm = mq8.reshape(dbsz, QPAD, nh, HEAD_DIM).transpose(0, 2, 1, 3).reshape(dbsz, nh * QPAD, HEAD_DIM)
        kmean, pm, pls, pacc = _moba_partials(pt_flat, mq_hm, cache_k, cache_v, l, dbsz, nb_past, n_pages)
        o_moba_s = _moba_merge(mq8, mk8, mv8, kmean, pm, pls, pacc, dbsz, dseq, nb_past)
        xs = _out_ln(o_gdn_s.reshape(ms, WIDTH), o_moba_s.reshape(ms, WIDTH), xs, gate_s, w_out16[l], lng, lnb,
                     ms, zero_idx, alpha)
        fbuf = state_ffn_conv[l]
        prev1 = jnp.broadcast_to(fbuf[:, 1:2, :], (dbsz, dseq, d_ff)).reshape(ms, d_ff)
        prev2 = jnp.concatenate([fbuf, jnp.zeros((dbsz, dseq - 2, d_ff), F32)], axis=1).reshape(ms, d_ff)
        xs, gfull = _ffn(xs, fscale_s, fshift_s, fgate_s, w_up16[l], w_down16[l], ffn_conv_w[l], ffn_conv_b[l][None],
                         flng, flnb, ms, tc_ffn, zero_idx, alpha, prev_rows=(prev1, prev2), dec_seq=dseq)
        gcs = jnp.concatenate([state_gdn_conv[l], gs3[:, :, :3 * WIDTH]], axis=1)[:, dseq:, :]
        fcs = jnp.concatenate([fbuf, gfull.reshape(dbsz, dseq, d_ff)], axis=1)[:, dseq:, :]

        for lst, val in zip(outs, (mk.reshape(bsz, seq, nh, HEAD_DIM), mv.reshape(bsz, seq, nh, HEAD_DIM),
                                   mk3.reshape(dbsz, dseq, nh, HEAD_DIM), mv3.reshape(dbsz, dseq, nh, HEAD_DIM),
                                   s_p, s_s, gcp, gcs, fcp, fcs)):
            lst.append(val)

    return (xp.reshape(bsz, seq, d), xs.reshape(dbsz, dseq, d)) + tuple(jnp.stack(o) for o in outs)


def kernel(x_prompt, x_sample, cache_k, cache_v, state_gdn, state_gdn_conv, state_ffn_conv, page_table, c_prompt, c_sample, w_ada_mix, b_ada_mix, w_in, gdn_conv_w, gdn_a_log, gdn_dt_bias, gdn_norm_g, w_out, ln_mix_g, ln_mix_b, w_ada_ffn, b_ada_ffn, w_up, ffn_conv_w, ffn_conv_b, w_down, ln_ffn_g, ln_ffn_b):
    return _forward(x_prompt, x_sample, cache_k, cache_v, state_gdn, state_gdn_conv, state_ffn_conv, page_table,
                    c_prompt, c_sample, w_ada_mix, b_ada_mix, w_in, gdn_conv_w, gdn_a_log, gdn_dt_bias, gdn_norm_g,
                    w_out, ln_mix_g, ln_mix_b, w_ada_ffn, b_ada_ffn, w_up, ffn_conv_w, ffn_conv_b, w_down,
                    ln_ffn_g, ln_ffn_b, past_len=8192, tm_prompt=512, tm_in=1024, tn_in=512, tc_ffn=512, tt_gdn=256)
```

```python
import functools
import math

import jax
import jax.numpy as jnp
from jax import lax
from jax.experimental import pallas as pl
from jax.experimental.pallas import tpu as pltpu

F32 = jnp.float32
MXU_DTYPE = jnp.bfloat16

HEAD_DIM = 128
LANES = 128
SUBLANES = 8
N_HEADS = 8
WIDTH = N_HEADS * HEAD_DIM
GDN_CONV = 4
FFN_CONV = 3
MOBA_BLOCK = 256
MOBA_TOPK = 3
PAGE_SIZE = 128
ROPE_THETA = 500000.0
ROPE_DIM = HEAD_DIM // 4
LN_EPS = 1e-5
NORM_EPS = 1e-6
GDN_CHUNK = 128
INV_BLOCK = 16
VMEM_LIMIT = 56 * 1024 * 1024
ROW_SUBTILE = 256
NEG_INF = float("-inf")


def _cparams(*sem):
    return pltpu.CompilerParams(dimension_semantics=sem, vmem_limit_bytes=VMEM_LIMIT)


def _mm(a, b):
    return jnp.dot(a.astype(MXU_DTYPE), b.astype(MXU_DTYPE), preferred_element_type=F32)


def _mm_nt(a, b):
    return lax.dot_general(a.astype(MXU_DTYPE), b.astype(MXU_DTYPE), (((1,), (1,)), ((), ())),
                           preferred_element_type=F32)


def _mm_tn(a, b):
    return lax.dot_general(a.astype(MXU_DTYPE), b.astype(MXU_DTYPE), (((0,), (0,)), ((), ())),
                           preferred_element_type=F32)


def _silu(x):
    return x * jax.nn.sigmoid(x)


def _layer_norm_rows(r, g, b):
    mu = jnp.mean(r, axis=-1, keepdims=True)
    d = r - mu
    var = jnp.mean(d * d, axis=-1, keepdims=True)
    return d * lax.rsqrt(var + LN_EPS) * g + b


def _adaln_kernel(c_ref, w_ref, b_ref, o_ref):
    o_ref[...] = _mm(c_ref[...], w_ref[...]) + b_ref[...]


def _adaln(c, w, b):
    n_layers, d, n = w.shape
    rows = c.shape[0]
    tn = 1024
    return pl.pallas_call(
        _adaln_kernel,
        out_shape=jax.ShapeDtypeStruct((n_layers, rows, n), F32),
        grid=(n_layers, n // tn),
        in_specs=[pl.BlockSpec((rows, d), lambda l, j: (0, 0)),
                  pl.BlockSpec((None, d, tn), lambda l, j: (l, 0, j)),
                  pl.BlockSpec((None, 1, tn), lambda l, j: (l, 0, j))],
        out_specs=pl.BlockSpec((None, rows, tn), lambda l, j: (l, 0, j)),
        compiler_params=_cparams("arbitrary", "arbitrary"),
        name="adaln",
    )(c, w, b.reshape(n_layers, 1, n))


def _rope_proj_store(h_ref, w_ref, rc_ref, rs1_ref, rs2_ref, o_ref):
    half = ROPE_DIM // 2
    tm = h_ref.shape[0]
    rs = min(tm, ROW_SUBTILE)
    subs = [slice(s * rs, (s + 1) * rs) for s in range(tm // rs)]
    ys = [jnp.dot(h_ref[sl, :], w_ref[...], preferred_element_type=F32) for sl in subs]
    for sl, y in zip(subs, ys):
        c, s1, s2 = rc_ref[sl, :], rs1_ref[sl, :], rs2_ref[sl, :]
        for h in range(y.shape[1] // HEAD_DIM):
            yh = y[:, h * HEAD_DIM:(h + 1) * HEAD_DIM]
            o_ref[sl, h * HEAD_DIM:(h + 1) * HEAD_DIM] = (
                yh * c + pltpu.roll(yh, HEAD_DIM - half, 1) * s1 + pltpu.roll(yh, half, 1) * s2)


def _in_proj_kernel(x_ref, sc_ref, sh_ref, w_ref, wba_ref, rc_ref, rs1_ref, rs2_ref,
                    g_ref, ba_ref, mq_ref, mk_ref, mv_ref, h_ref, *, n_g, n_m):
    j = pl.program_id(1)

    @pl.when(j == 0)
    def _():
        h_ref[...] = (x_ref[...] * (1.0 + sc_ref[...]) + sh_ref[...]).astype(h_ref.dtype)
        ba_ref[...] = jnp.dot(h_ref[...], wba_ref[...], preferred_element_type=F32)

    def proj():
        return jnp.dot(h_ref[...], w_ref[...], preferred_element_type=F32)

    @pl.when(j < n_g)
    def _():
        g_ref[...] = proj()

    @pl.when((j >= n_g) & (j < n_g + n_m))
    def _():
        _rope_proj_store(h_ref, w_ref, rc_ref, rs1_ref, rs2_ref, mq_ref)

    @pl.when((j >= n_g + n_m) & (j < n_g + 2 * n_m))
    def _():
        _rope_proj_store(h_ref, w_ref, rc_ref, rs1_ref, rs2_ref, mk_ref)

    @pl.when(j >= n_g + 2 * n_m)
    def _():
        mv_ref[...] = proj()


def _mod_spec(mod, layer, part, index, n_grid):
    r, d = mod.shape[3], mod.shape[4]
    if n_grid == 1:
        return pl.BlockSpec((None, None, None, r, d), lambda i: (layer, part, index(i), 0, 0))
    return pl.BlockSpec((None, None, None, r, d), lambda i, j: (layer, part, index(i), 0, 0))


def _in_proj(x, mod, w_all, w_ba, rope, layer, tm, tn, mod_index, rope_index):
    m, d = x.shape
    n_g, n_m = 4 * WIDTH // tn, WIDTH // tn
    rope_spec = pl.BlockSpec((tm, LANES), lambda i, j: (rope_index(i), 0))

    def m_spec(first):
        return pl.BlockSpec((tm, tn), lambda i, j: (i, jnp.clip(j - first, 0, n_m - 1)))

    return pl.pallas_call(
        functools.partial(_in_proj_kernel, n_g=n_g, n_m=n_m),
        out_shape=(jax.ShapeDtypeStruct((m, 4 * WIDTH), F32),
                   jax.ShapeDtypeStruct((m, LANES), F32),
                   jax.ShapeDtypeStruct((m, WIDTH), F32),
                   jax.ShapeDtypeStruct((m, WIDTH), F32),
                   jax.ShapeDtypeStruct((m, WIDTH), F32)),
        grid=(m // tm, n_g + 3 * n_m),
        in_specs=[pl.BlockSpec((tm, d), lambda i, j: (i, 0)),
                  _mod_spec(mod, layer, 1, mod_index, 2), _mod_spec(mod, layer, 0, mod_index, 2),
                  pl.BlockSpec((None, d, tn), lambda i, j: (layer, 0, j)),
                  pl.BlockSpec((None, d, LANES), lambda i, j: (layer, 0, 0)),
                  rope_spec, rope_spec, rope_spec],
        out_specs=(pl.BlockSpec((tm, tn), lambda i, j: (i, jnp.minimum(j, n_g - 1))),
                   pl.BlockSpec((tm, LANES), lambda i, j: (i, 0)),
                   m_spec(n_g), m_spec(n_g + n_m), m_spec(n_g + 2 * n_m)),
        scratch_shapes=[pltpu.VMEM((tm, d), MXU_DTYPE)],
        compiler_params=_cparams("arbitrary", "arbitrary"),
        name="in_proj",
    )(x, mod, mod, w_all, w_ba, *rope)


def _unit_lower_inverse(a_low, eye, same_blk):
    items = range(len(a_low))
    d = [jnp.where(same_blk, -a, 0.0) for a in a_low]
    o = [-a_low[i] - d[i] for i in items]
    x = [eye + d[i] for i in items]
    p = d
    for _ in range(int(math.log2(INV_BLOCK)) - 1):
        p = [_mm(p[i], p[i]) for i in items]
        x = [x[i] + _mm(x[i], p[i]) for i in items]
    y = [_mm(x[i], o[i]) for i in items]
    z = [eye + y[i] for i in items]
    p = y
    for _ in range(int(math.log2(GDN_CHUNK // INV_BLOCK)) - 1):
        p = [_mm(p[i], p[i]) for i in items]
        z = [z[i] + _mm(z[i], p[i]) for i in items]
    return z, x


def _gdn_wy(q, k, v, beta_c, eg_c, gc_col, gc_row, masks):
    eye, same_blk, tri_incl, tri_strict = masks
    items = range(len(q))
    c = q[0].shape[0]
    kb = [k[i] * beta_c[i] for i in items]
    p = [_mm_nt(jnp.concatenate([q[i], kb[i]], axis=0), k[i]) for i in items]
    decay = [jnp.exp(jnp.where(tri_incl, gc_col[i] - gc_row[i], NEG_INF)) for i in items]
    qk = [p[i][:c] * decay[i] for i in items]
    a_low = [jnp.where(tri_strict, p[i][c:] * decay[i], 0.0) for i in items]
    z, x = _unit_lower_inverse(a_low, eye, same_blk)
    rhs = [jnp.concatenate([v[i] * beta_c[i], kb[i] * eg_c[i]], axis=1) for i in items]
    t = [_mm(x[i], rhs[i]) for i in items]
    sol = [_mm(z[i], t[i]) for i in items]
    return [s[:, :HEAD_DIM] for s in sol], [s[:, HEAD_DIM:] for s in sol], qk


def _gdn_state_step(q, k, u, w, qk, eg_c, ek_c, gl, s):
    items = range(len(q))
    v_new = [u[i] - _mm(w[i], s[i]) for i in items]
    o = [_mm(q[i] * eg_c[i], s[i]) + _mm(qk[i], v_new[i]) for i in items]
    s_new = [s[i] * gl[i] + _mm_tn(k[i] * ek_c[i], v_new[i]) for i in items]
    return o, s_new


def _gdn_kernel(x_ref, ba_ref, cw_ref, alog_ref, dtb_ref, ng_ref, *rest, tt, n_valid, from_state):
    if from_state:
        cs_ref, s0_ref, o_ref, sout_ref, xbuf, zbuf, babuf, s_scr = rest
    else:
        o_ref, sout_ref, xbuf, s_scr = rest
    t = pl.program_id(1)
    nk = GDN_CONV - 1
    c = GDN_CHUNK

    if from_state:
        xbuf[...] = jnp.zeros(xbuf.shape, F32)
        xbuf[SUBLANES - nk:SUBLANES, :] = cs_ref[...]
        xbuf[SUBLANES:SUBLANES + n_valid, :] = x_ref[:, 0:3 * WIDTH]
        zbuf[...] = jnp.zeros(zbuf.shape, F32)
        zbuf[0:n_valid, :] = x_ref[:, 3 * WIDTH:4 * WIDTH]
        babuf[...] = jnp.zeros(babuf.shape, F32)
        babuf[0:n_valid, :] = ba_ref[...]
        s_scr[...] = s0_ref[...]
        ba = babuf[...]
    else:
        @pl.when(t == 0)
        def _():
            xbuf[0:SUBLANES, :] = jnp.zeros((SUBLANES, 3 * WIDTH), F32)
            s_scr[...] = jnp.zeros(s_scr.shape, F32)

        @pl.when(t > 0)
        def _():
            xbuf[0:SUBLANES, :] = xbuf[tt:tt + SUBLANES, :]

        xbuf[SUBLANES:SUBLANES + tt, :] = x_ref[:, 0:3 * WIDTH]
        ba = ba_ref[...]

    beta_all = jax.nn.sigmoid(ba)
    a_in = ba + dtb_ref[...]
    softplus = jnp.maximum(a_in, 0.0) + jnp.log1p(jnp.exp(-jnp.abs(a_in)))
    g_all = -jnp.exp(alog_ref[...]) * softplus
    if n_valid < tt:
        live = lax.broadcasted_iota(jnp.int32, (tt, LANES), 0) < n_valid
        beta_all = jnp.where(live, beta_all, 0.0)
        g_all = jnp.where(live, g_all, 0.0)

    row = lax.broadcasted_iota(jnp.int32, (c, c), 0)
    col = lax.broadcasted_iota(jnp.int32, (c, c), 1)
    tri_incl = row >= col
    tri_strict = row > col
    masks = (jnp.where(row == col, 1.0, 0.0).astype(F32), (row // INV_BLOCK) == (col // INV_BLOCK),
             tri_incl, tri_strict)
    ltri = jnp.where(tri_incl, 1.0, 0.0).astype(F32)

    def conv_act(col0):
        acc = xbuf[SUBLANES - nk:SUBLANES - nk + tt, col0:col0 + HEAD_DIM] * cw_ref[0:1, col0:col0 + HEAD_DIM]
        for i in range(1, GDN_CONV):
            acc = acc + (xbuf[SUBLANES - nk + i:SUBLANES - nk + i + tt, col0:col0 + HEAD_DIM]
                         * cw_ref[i:i + 1, col0:col0 + HEAD_DIM])
        return _silu(acc)

    def l2n(a):
        return a * lax.rsqrt(jnp.sum(a * a, axis=-1, keepdims=True) + NORM_EPS)

    chunk_decay = []
    for ci in range(tt // c):
        gcum = jnp.dot(ltri, g_all[ci * c:(ci + 1) * c, :], precision=lax.Precision.HIGHEST,
                       preferred_element_type=F32)
        chunk_decay.append((gcum, gcum.T, jnp.exp(gcum), jnp.exp(gcum[c - 1:c, :] - gcum)))

    heads = range(N_HEADS)
    n_chunks = tt // c
    q_t = [l2n(conv_act(h * HEAD_DIM)) * (HEAD_DIM ** -0.5) for h in heads]
    k_t = [l2n(conv_act(WIDTH + h * HEAD_DIM)) for h in heads]
    v_t = [conv_act(2 * WIDTH + h * HEAD_DIM) for h in heads]

    pairs = [(ci, h) for ci in range(n_chunks) for h in heads]
    rows = lambda a, ci: a[ci * c:(ci + 1) * c]
    lane = lambda a, h: a[:, N_HEADS + h:N_HEADS + h + 1]
    qs = [rows(q_t[h], ci) for ci, h in pairs]
    ks = [rows(k_t[h], ci) for ci, h in pairs]
    egs = [lane(chunk_decay[ci][2], h) for ci, h in pairs]
    u, w, qk = _gdn_wy(
        qs, ks, [rows(v_t[h], ci) for ci, h in pairs],
        [rows(beta_all, ci)[:, h:h + 1] for ci, h in pairs], egs,
        [lane(chunk_decay[ci][0], h) for ci, h in pairs],
        [chunk_decay[ci][1][N_HEADS + h:N_HEADS + h + 1, :] for ci, h in pairs], masks)

    s = [s_scr[h] for h in heads]
    for ci in range(n_chunks):
        sl = slice(ci * N_HEADS, (ci + 1) * N_HEADS)
        eks = [lane(chunk_decay[ci][3], h) for h in heads]
        gls = [lane(chunk_decay[ci][2], h)[c - 1:c] for h in heads]
        o, s = _gdn_state_step(qs[sl], ks[sl], u[sl], w[sl], qk[sl], egs[sl], eks, gls, s)
        r0 = ci * c
        for h in heads:
            hs = slice(h * HEAD_DIM, (h + 1) * HEAD_DIM)
            if from_state:
                zh = zbuf[r0:r0 + c, hs]
            else:
                zh = x_ref[r0:r0 + c, 3 * WIDTH + h * HEAD_DIM:3 * WIDTH + (h + 1) * HEAD_DIM]
            on = o[h] * lax.rsqrt(jnp.mean(o[h] * o[h], axis=-1, keepdims=True) + NORM_EPS) * ng_ref[...] * _silu(zh)
            if from_state:
                o_ref[:, hs] = on[0:n_valid].astype(o_ref.dtype)
            else:
                o_ref[r0:r0 + c, hs] = on.astype(o_ref.dtype)
    for h in heads:
        s_scr[h] = s[h]
        sout_ref[h] = s[h]


def _gdn_prompt(g, ba, conv_w, alog_pad, dtb_pad, norm_g, layer, batch, seq, tt):
    m = g.shape[0]
    nt = seq // tt
    kern = functools.partial(_gdn_kernel, tt=tt, n_valid=tt, from_state=False)
    small = lambda shape: pl.BlockSpec((None,) + shape, lambda b, t: (layer, 0, 0))
    return pl.pallas_call(
        kern,
        out_shape=(jax.ShapeDtypeStruct((m, WIDTH), MXU_DTYPE),
                   jax.ShapeDtypeStruct((batch, N_HEADS, HEAD_DIM, HEAD_DIM), F32)),
        grid=(batch, nt),
        in_specs=[pl.BlockSpec((tt, 4 * WIDTH), lambda b, t: (b * nt + t, 0)),
                  pl.BlockSpec((tt, LANES), lambda b, t: (b * nt + t, 0)),
                  small((GDN_CONV, 3 * WIDTH)), small((1, LANES)), small((1, LANES)), small((1, HEAD_DIM))],
        out_specs=(pl.BlockSpec((tt, WIDTH), lambda b, t: (b * nt + t, 0)),
                   pl.BlockSpec((None, N_HEADS, HEAD_DIM, HEAD_DIM), lambda b, t: (b, 0, 0, 0))),
        scratch_shapes=[pltpu.VMEM((tt + SUBLANES, 3 * WIDTH), F32),
                        pltpu.VMEM((N_HEADS, HEAD_DIM, HEAD_DIM), F32)],
        compiler_params=_cparams("arbitrary", "arbitrary"),
        name="gdn_prompt",
    )(g, ba, conv_w, alog_pad, dtb_pad, norm_g)


def _gdn_sample(g, ba, conv_state, s0, conv_w, alog_pad, dtb_pad, norm_g, layer, batch, dec_seq):
    tt = GDN_CHUNK
    kern = functools.partial(_gdn_kernel, tt=tt, n_valid=dec_seq, from_state=True)
    small = lambda shape: pl.BlockSpec((None,) + shape, lambda b, t: (layer, 0, 0))
    return pl.pallas_call(
        kern,
        out_shape=(jax.ShapeDtypeStruct((batch, dec_seq, WIDTH), F32),
                   jax.ShapeDtypeStruct((batch, N_HEADS, HEAD_DIM, HEAD_DIM), F32)),
        grid=(batch, 1),
        in_specs=[pl.BlockSpec((None, dec_seq, 4 * WIDTH), lambda b, t: (b, 0, 0)),
                  pl.BlockSpec((None, dec_seq, LANES), lambda b, t: (b, 0, 0)),
                  small((GDN_CONV, 3 * WIDTH)), small((1, LANES)), small((1, LANES)), small((1, HEAD_DIM)),
                  pl.BlockSpec((None, None, GDN_CONV - 1, 3 * WIDTH), lambda b, t: (layer, b, 0, 0)),
                  pl.BlockSpec((None, None, N_HEADS, HEAD_DIM, HEAD_DIM), lambda b, t: (layer, b, 0, 0, 0))],
        out_specs=(pl.BlockSpec((None, dec_seq, WIDTH), lambda b, t: (b, 0, 0)),
                   pl.BlockSpec((None, N_HEADS, HEAD_DIM, HEAD_DIM), lambda b, t: (b, 0, 0, 0))),
        scratch_shapes=[pltpu.VMEM((tt + SUBLANES, 3 * WIDTH), F32),
                        pltpu.VMEM((tt, WIDTH), F32),
                        pltpu.VMEM((tt, LANES), F32),
                        pltpu.VMEM((N_HEADS, HEAD_DIM, HEAD_DIM), F32)],
        compiler_params=_cparams("arbitrary", "arbitrary"),
        name="gdn_sample",
    )(g, ba, conv_w, alog_pad, dtb_pad, norm_g, conv_state, s0)


def _topk_select(gate, n_valid, idx, axis):
    n = gate.shape[axis]
    cnt = jnp.zeros(gate.shape, F32)
    for j in range(n):
        cand = lax.slice_in_dim(gate, j, j + 1, axis=axis)
        beats = jnp.where(cand > gate, 1.0, jnp.where((cand == gate) & (j < idx), 1.0, 0.0))
        cnt = cnt + jnp.where(j < n_valid, beats, 0.0)
    return (cnt < MOBA_TOPK) & (idx < n_valid)


MOBA_HEADS_PER_STEP = 4


def _moba_prompt_kernel(q_ref, k_ref, v_ref, o_ref, k16, vt16, kmean, *, nb):
    i = pl.program_id(2)
    blk = MOBA_BLOCK
    scale = HEAD_DIM ** -0.5
    group = range(MOBA_HEADS_PER_STEP)
    lanes = lambda g: slice(g * HEAD_DIM, (g + 1) * HEAD_DIM)

    @pl.when(i == 0)
    def _():
        for g in group:
            for j in range(nb):
                kj = k_ref[j * blk:(j + 1) * blk, lanes(g)]
                k16[g, j] = kj.astype(MXU_DTYPE)
                kmean[g, j:j + 1, :] = jnp.mean(kj, axis=0, keepdims=True)
                vt16[g, j] = v_ref[j * blk:(j + 1) * blk, lanes(g)].T.astype(MXU_DTYPE)

    kpos = lax.broadcasted_iota(jnp.int32, (blk, blk), 0)
    qpos = lax.broadcasted_iota(jnp.int32, (blk, blk), 1)

    def attend(cur):
        qt = [q_ref[:, lanes(g)].T for g in group]
        qt16 = [a.astype(MXU_DTYPE) for a in qt]
        if cur > MOBA_TOPK:
            gate = [jnp.dot(kmean[g], qt[g], precision=lax.Precision.HIGHEST, preferred_element_type=F32)
                    for g in group]
            blk_idx = lax.broadcasted_iota(jnp.int32, gate[0].shape, 0)
            sel = [jnp.where(_topk_select(gate[g], cur, blk_idx, 0), 1.0, 0.0) for g in group]
        s = [[] for _ in group]
        for j in range(cur + 1):
            for g in group:
                sj = jnp.dot(k16[g, j], qt16[g], preferred_element_type=F32) * scale
                if j == cur:
                    sj = jnp.where(kpos <= qpos, sj, NEG_INF)
                elif cur > MOBA_TOPK:
                    sj = jnp.where(sel[g][j:j + 1, :] > 0.5, sj, NEG_INF)
                s[g].append(sj)
        m = [functools.reduce(jnp.maximum, [jnp.max(sj, axis=0, keepdims=True) for sj in s[g]]) for g in group]
        l = [jnp.zeros((1, blk), F32) for _ in group]
        acc = [jnp.zeros((HEAD_DIM, blk), F32) for _ in group]
        for j in range(cur + 1):
            for g in group:
                p = jnp.exp(s[g][j] - m[g])
                l[g] = l[g] + jnp.sum(p, axis=0, keepdims=True)
                acc[g] = acc[g] + jnp.dot(vt16[g, j], p.astype(MXU_DTYPE), preferred_element_type=F32)
        for g in group:
            o_ref[:, lanes(g)] = (acc[g] / l[g]).T.astype(o_ref.dtype)

    for cur in range(nb):
        pl.when(i == cur)(functools.partial(attend, cur))


def _moba_prompt(mq, mk, mv, batch, seq):
    m = mq.shape[0]
    nb = seq // MOBA_BLOCK
    hps = MOBA_HEADS_PER_STEP
    kern = functools.partial(_moba_prompt_kernel, nb=nb)
    kv_spec = pl.BlockSpec((seq, hps * HEAD_DIM), lambda b, h, i: (b, h))
    q_spec = pl.BlockSpec((MOBA_BLOCK, hps * HEAD_DIM), lambda b, h, i: (b * nb + i, h))
    return pl.pallas_call(
        kern,
        out_shape=jax.ShapeDtypeStruct((m, WIDTH), MXU_DTYPE),
        grid=(batch, N_HEADS // hps, nb),
        in_specs=[q_spec, kv_spec, kv_spec],
        out_specs=q_spec,
        scratch_shapes=[pltpu.VMEM((hps, nb, MOBA_BLOCK, HEAD_DIM), MXU_DTYPE),
                        pltpu.VMEM((hps, nb, HEAD_DIM, MOBA_BLOCK), MXU_DTYPE),
                        pltpu.VMEM((hps, nb, HEAD_DIM), F32)],
        compiler_params=_cparams("arbitrary", "arbitrary", "arbitrary"),
        name="moba_prompt",
    )(mq, mk, mv)


QPAD = SUBLANES


PAST_BLOCKS_PER_STEP = 8


def _moba_partial_kernel(pt_ref, q_ref, *refs):
    nblk, ppb = PAST_BLOCKS_PER_STEP, MOBA_BLOCK // PAGE_SIZE
    k_refs, v_refs = refs[:nblk * ppb], refs[nblk * ppb:2 * nblk * ppb]
    kmean_ref, pm_ref, pl_ref, pacc_ref = refs[2 * nblk * ppb:]
    n = pl.program_id(1)
    scale = HEAD_DIM ** -0.5
    rows = PAGE_SIZE * N_HEADS
    q16 = q_ref[...].astype(MXU_DTYPE)
    q_head = lax.broadcasted_iota(jnp.int32, (N_HEADS * QPAD, rows), 0) // QPAD
    k_head = lax.broadcasted_iota(jnp.int32, (N_HEADS * QPAD, rows), 1) % N_HEADS
    same_head = q_head == k_head
    for g in range(nblk):
        ks = [k_refs[g * ppb + pg][...] for pg in range(ppb)]
        kmean_ref[n * nblk + g] = sum(jnp.sum(k, axis=0) for k in ks) * (1.0 / MOBA_BLOCK)
        s = [jnp.where(same_head, _mm_nt(q16, k.reshape(rows, HEAD_DIM)) * scale, NEG_INF) for k in ks]
        mx = functools.reduce(jnp.maximum, [jnp.max(sp, axis=-1, keepdims=True) for sp in s])
        p = [jnp.exp(sp - mx) for sp in s]
        lsum = sum(jnp.sum(pp, axis=-1, keepdims=True) for pp in p)
        pm_ref[g] = jnp.broadcast_to(mx, pm_ref.shape[1:])
        pl_ref[g] = jnp.broadcast_to(lsum, pl_ref.shape[1:])
        pacc_ref[g] = sum(_mm(p[pg], v_refs[g * ppb + pg][...].reshape(rows, HEAD_DIM)) for pg in range(ppb))


def _moba_partials(page_table_flat, q, cache_k, cache_v, layer, batch, nb, n_pages):
    nblk, ppb = PAST_BLOCKS_PER_STEP, MOBA_BLOCK // PAGE_SIZE
    assert nb % nblk == 0
    hq = N_HEADS * QPAD

    def page_spec(g, pg):
        return pl.BlockSpec((None, None, PAGE_SIZE, N_HEADS, HEAD_DIM),
                            lambda b, n, pt: (layer, pt[b * n_pages + ppb * (n * nblk + g) + pg], 0, 0, 0))

    page_specs = [page_spec(g, pg) for g in range(nblk) for pg in range(ppb)]
    part_shape = jax.ShapeDtypeStruct((batch, nb, hq, LANES), F32)
    part_spec = pl.BlockSpec((None, nblk, hq, LANES), lambda b, n, pt: (b, n, 0, 0))
    return pl.pallas_call(
        _moba_partial_kernel,
        out_shape=(jax.ShapeDtypeStruct((batch, nb, N_HEADS, HEAD_DIM), F32), part_shape, part_shape, part_shape),
        grid_spec=pltpu.PrefetchScalarGridSpec(
            num_scalar_prefetch=1,
            grid=(batch, nb // nblk),
            in_specs=[pl.BlockSpec((None, hq, HEAD_DIM), lambda b, n, pt: (b, 0, 0))] + page_specs + page_specs,
            out_specs=(pl.BlockSpec((None, nb, N_HEADS, HEAD_DIM), lambda b, n, pt: (b, 0, 0, 0)),
                       part_spec, part_spec, part_spec)),
        compiler_params=_cparams("arbitrary", "arbitrary"),
        name="moba_sample_partials",
    )(page_table_flat, q, *([cache_k] * (nblk * ppb)), *([cache_v] * (nblk * ppb)))


def _moba_merge_kernel(q_ref, k_ref, v_ref, kmean_ref, pm_ref, pl_ref, pacc_ref, o_ref, *, dec_seq, nb):
    scale = HEAD_DIM ** -0.5
    q8, k8, v8 = q_ref[...], k_ref[...], v_ref[...]
    qi = lax.broadcasted_iota(jnp.int32, (QPAD, QPAD), 0)
    kj = lax.broadcasted_iota(jnp.int32, (QPAD, QPAD), 1)
    own_ok = (kj <= qi) & (kj < dec_seq)
    blk_idx = lax.broadcasted_iota(jnp.int32, (QPAD, nb), 1)
    for h in range(N_HEADS):
        hs = slice(h * HEAD_DIM, (h + 1) * HEAD_DIM)
        hq = slice(h * QPAD, (h + 1) * QPAD)
        gate = lax.dot_general(q8[:, hs], kmean_ref[:, h, :], (((1,), (1,)), ((), ())),
                               precision=lax.Precision.HIGHEST, preferred_element_type=F32)
        sel = jnp.where(_topk_select(gate, nb, blk_idx, 1), 1.0, 0.0)
        s_own = jnp.where(own_ok, _mm_nt(q8[:, hs], k8[:, hs]) * scale, NEG_INF)
        m_tot = jnp.broadcast_to(jnp.max(s_own, axis=-1, keepdims=True), (QPAD, LANES))
        for n in range(nb):
            m_tot = jnp.maximum(m_tot, jnp.where(sel[:, n:n + 1] > 0.5, pm_ref[n, hq, :], NEG_INF))
        p_own = jnp.exp(s_own - m_tot[:, 0:1])
        l_tot = jnp.broadcast_to(jnp.sum(p_own, axis=-1, keepdims=True), (QPAD, LANES))
        acc = jnp.zeros((QPAD, HEAD_DIM), F32)
        for j in range(dec_seq):
            acc = acc + p_own[:, j:j + 1] * v8[j:j + 1, hs]
        for n in range(nb):
            w = jnp.where(sel[:, n:n + 1] > 0.5, jnp.exp(pm_ref[n, hq, :] - m_tot), 0.0)
            l_tot = l_tot + w * pl_ref[n, hq, :]
            acc = acc + w * pacc_ref[n, hq, :]
        o_ref[:, hs] = (acc / l_tot)[0:dec_seq].astype(o_ref.dtype)


def _moba_merge(q, k, v, kmean, pm, plsum, pacc, batch, dec_seq, nb):
    kern = functools.partial(_moba_merge_kernel, dec_seq=dec_seq, nb=nb)
    row_spec = pl.BlockSpec((None, QPAD, WIDTH), lambda b: (b, 0, 0))
    part_spec = pl.BlockSpec((None, nb, N_HEADS * QPAD, LANES), lambda b: (b, 0, 0, 0))
    return pl.pallas_call(
        kern,
        out_shape=jax.ShapeDtypeStruct((batch, dec_seq, WIDTH), F32),
        grid=(batch,),
        in_specs=[row_spec, row_spec, row_spec,
                  pl.BlockSpec((None, nb, N_HEADS, HEAD_DIM), lambda b: (b, 0, 0, 0)),
                  part_spec, part_spec, part_spec],
        out_specs=pl.BlockSpec((None, dec_seq, WIDTH), lambda b: (b, 0, 0)),
        compiler_params=_cparams("arbitrary"),
        name="moba_sample_merge",
    )(q, k, v, kmean, pm, plsum, pacc)


def _out_ln_kernel(og_ref, om_ref, x_ref, gate_ref, w1_ref, w2_ref, lng_ref, lnb_ref, o_ref, *, alpha):
    tm = x_ref.shape[0]
    rs = min(tm, ROW_SUBTILE)
    subs = [slice(s * rs, (s + 1) * rs) for s in range(tm // rs)]
    y = [_mm(og_ref[sl, :], w1_ref[...]) + _mm(om_ref[sl, :], w2_ref[...]) for sl in subs]
    for sl, ys in zip(subs, y):
        gate = gate_ref[...] if gate_ref.shape[0] == 1 else gate_ref[sl, :]
        r = alpha * x_ref[sl, :] + (1.0 + gate) * ys
        o_ref[sl, :] = _layer_norm_rows(r, lng_ref[...], lnb_ref[...])


def _out_ln(o_gdn, o_moba, x, mod, w_out, ln_g, ln_b, layer, tm, mod_index, alpha):
    m, d = x.shape
    vec = pl.BlockSpec((None, 1, d), lambda i: (layer, 0, 0))
    return pl.pallas_call(
        functools.partial(_out_ln_kernel, alpha=alpha),
        out_shape=jax.ShapeDtypeStruct((m, d), F32),
        grid=(m // tm,),
        in_specs=[pl.BlockSpec((tm, WIDTH), lambda i: (i, 0)), pl.BlockSpec((tm, WIDTH), lambda i: (i, 0)),
                  pl.BlockSpec((tm, d), lambda i: (i, 0)),
                  _mod_spec(mod, layer, 2, mod_index, 1),
                  pl.BlockSpec((None, WIDTH, d), lambda i: (layer, 0, 0)),
                  pl.BlockSpec((None, WIDTH, d), lambda i: (layer, 1, 0)),
                  vec, vec],
        out_specs=pl.BlockSpec((tm, d), lambda i: (i, 0)),
        compiler_params=_cparams("arbitrary"),
        name="out_proj_ln",
    )(o_gdn, o_moba, x, mod, w_out, w_out, ln_g, ln_b)


def _ffn_kernel(x_ref, sc_ref, sh_ref, gate_ref, wg_ref, wu_ref, wd_ref, cw_ref, cb_ref, lng_ref, lnb_ref, *rest,
                tm, n_sub, n_chunks, tiles_per_seq, dec_seq, alpha):
    sample = dec_seq is not None
    if sample:
        p1_ref, p2_ref, o_ref, st_ref, h_ref, acc_ref, gbuf = rest
    else:
        o_ref, st_ref, h_ref, acc_ref, gbuf, carry = rest
    i = pl.program_id(0)
    c = pl.program_id(1)
    tc = wg_ref.shape[1]
    rs = tm // n_sub
    subs = [slice(s * rs, (s + 1) * rs) for s in range(n_sub)]
    mod_rows = lambda ref, sl: ref[...] if ref.shape[0] == 1 else ref[sl, :]

    def chunk_step(first_chunk, last_chunk):
        if sample:
            gbuf[0:SUBLANES, :] = jnp.zeros((SUBLANES, tc), F32)
        else:
            seq_start = (i % tiles_per_seq) == 0

            @pl.when(seq_start)
            def _():
                gbuf[0:SUBLANES, :] = jnp.zeros((SUBLANES, tc), F32)

            @pl.when(jnp.logical_not(seq_start))
            def _():
                gbuf[0:SUBLANES, :] = carry[c]

        g, u = [], []
        for s, sl in enumerate(subs):
            if first_chunk:
                h_ref[sl, :] = (x_ref[sl, :] * (1.0 + mod_rows(sc_ref, sl)) + mod_rows(sh_ref, sl)).astype(h_ref.dtype)
            g.append(jnp.dot(h_ref[sl, :], wg_ref[...], preferred_element_type=F32))
            gbuf[SUBLANES + s * rs:SUBLANES + (s + 1) * rs, :] = g[s]
            u.append(jnp.dot(h_ref[sl, :], wu_ref[...], preferred_element_type=F32))
        if not sample:
            carry[c] = g[-1][rs - SUBLANES:rs, :]
            st_ref[...] = g[-1][rs - (FFN_CONV - 1):rs, :]
        for s, sl in enumerate(subs):
            prev1 = gbuf[SUBLANES - 1 + s * rs:SUBLANES - 1 + (s + 1) * rs, :]
            prev2 = gbuf[SUBLANES - 2 + s * rs:SUBLANES - 2 + (s + 1) * rs, :]
            if sample:
                pos = lax.broadcasted_iota(jnp.int32, (rs, tc), 0) % dec_seq
                prev1 = jnp.where(pos >= 1, prev1, p1_ref[sl, :])
                prev2 = jnp.where(pos >= 2, prev2, p2_ref[sl, :])
                st_ref[sl, :] = g[s]
            conv = prev2 * cw_ref[0:1, :] + prev1 * cw_ref[1:2, :] + g[s] * cw_ref[2:3, :]
            act = _silu(conv + cb_ref[...]) * u[s]
            down = jnp.dot(act.astype(MXU_DTYPE), wd_ref[...], preferred_element_type=F32)
            total = down if first_chunk else acc_ref[sl, :] + down
            if last_chunk:
                r = alpha * x_ref[sl, :] + (1.0 + mod_rows(gate_ref, sl)) * total
                o_ref[sl, :] = _layer_norm_rows(r, lng_ref[...], lnb_ref[...])
            else:
                acc_ref[sl, :] = total

    assert n_chunks >= 2
    pl.when(c == 0)(functools.partial(chunk_step, True, False))
    pl.when((c > 0) & (c < n_chunks - 1))(functools.partial(chunk_step, False, False))
    pl.when(c == n_chunks - 1)(functools.partial(chunk_step, False, True))


def _ffn(x, mod, w_up, w_down, conv_w, conv_b, ln_g, ln_b, layer, tm, tc, mod_index, alpha,
         tiles_per_seq=None, prev_rows=None, dec_seq=None):
    m, d = x.shape
    d_ff = w_down.shape[1]
    n_chunks = d_ff // tc
    sample = prev_rows is not None
    vec = pl.BlockSpec((None, 1, d), lambda i, c: (layer, 0, 0))
    in_specs = [pl.BlockSpec((tm, d), lambda i, c: (i, 0)),
                _mod_spec(mod, layer, 1, mod_index, 2), _mod_spec(mod, layer, 0, mod_index, 2),
                _mod_spec(mod, layer, 2, mod_index, 2),
                pl.BlockSpec((None, d, tc), lambda i, c: (layer, 0, c)),
                pl.BlockSpec((None, d, tc), lambda i, c: (layer, 0, n_chunks + c)),
                pl.BlockSpec((None, tc, d), lambda i, c: (layer, c, 0)),
                pl.BlockSpec((None, FFN_CONV, tc), lambda i, c: (layer, 0, c)),
                pl.BlockSpec((None, 1, tc), lambda i, c: (layer, 0, c)),
                vec, vec]
    args = [x, mod, mod, mod, w_up, w_up, w_down, conv_w, conv_b, ln_g, ln_b]
    scratch = [pltpu.VMEM((tm, d), MXU_DTYPE), pltpu.VMEM((tm, d), F32), pltpu.VMEM((tm + SUBLANES, tc), F32)]
    if sample:
        in_specs += [pl.BlockSpec((None, tm, tc), lambda i, c: (layer, i, c))] * 2
        args += list(prev_rows)
        st_shape = jax.ShapeDtypeStruct((m, d_ff), F32)
        st_spec = pl.BlockSpec((tm, tc), lambda i, c: (i, c))
    else:
        st_shape = jax.ShapeDtypeStruct((m // tm, FFN_CONV - 1, d_ff), F32)
        st_spec = pl.BlockSpec((None, FFN_CONV - 1, tc), lambda i, c: (i, 0, c))
        scratch.append(pltpu.VMEM((n_chunks, SUBLANES, tc), F32))
    kern = functools.partial(_ffn_kernel, tm=tm, n_sub=max(1, tm // ROW_SUBTILE), n_chunks=n_chunks,
                             tiles_per_seq=tiles_per_seq,
                             dec_seq=dec_seq if sample else None, alpha=alpha)
    return pl.pallas_call(
        kern,
        out_shape=(jax.ShapeDtypeStruct((m, d), F32), st_shape),
        grid=(m // tm, n_chunks),
        in_specs=in_specs,
        out_specs=(pl.BlockSpec((tm, d), lambda i, c: (i, 0)), st_spec),
        scratch_shapes=scratch,
        compiler_params=_cparams("arbitrary", "arbitrary"),
        name="ffn_sample" if sample else "ffn_prompt",
    )(*args)


def _rope_tables(pos):
    half = ROPE_DIM // 2
    inv_freq = ROPE_THETA ** (-jnp.arange(half, dtype=F32) / half)
    ang = pos.astype(F32)[:, None] * inv_freq[None, :]
    cos, sin = jnp.cos(ang), jnp.sin(ang)
    n = pos.shape[0]
    ones = jnp.ones((n, HEAD_DIM - ROPE_DIM), F32)
    zeros = jnp.zeros((n, HEAD_DIM - ROPE_DIM), F32)
    zh = jnp.zeros((n, half), F32)
    return (jnp.concatenate([cos, cos, ones], axis=1),
            jnp.concatenate([-sin, zh, zeros], axis=1),
            jnp.concatenate([zh, sin, zeros], axis=1))


def _forward(x_prompt, x_sample, cache_k, cache_v, state_gdn, state_gdn_conv, state_ffn_conv, page_table,
             c_prompt, c_sample, w_ada_mix, b_ada_mix, w_in, gdn_conv_w, gdn_a_log, gdn_dt_bias, gdn_norm_g,
             w_out, ln_mix_g, ln_mix_b, w_ada_ffn, b_ada_ffn, w_up, ffn_conv_w, ffn_conv_b, w_down,
             ln_ffn_g, ln_ffn_b, *, past_len, tm_prompt, tm_in, tn_in, tc_ffn, tt_gdn):
    bsz, seq, d = x_prompt.shape
    dbsz, dseq, _ = x_sample.shape
    depth = w_in.shape[0]
    d_ff = w_down.shape[1]
    alpha = (2 * depth) ** 0.25
    n_pool = cache_k.shape[1]
    n_pages = page_table.shape[1]
    nb_past = past_len // MOBA_BLOCK
    assert d == 2 * WIDTH and past_len % MOBA_BLOCK == 0 and nb_past >= MOBA_TOPK
    assert n_pages * PAGE_SIZE == past_len and seq % MOBA_BLOCK == 0
    mp, ms = bsz * seq, dbsz * dseq
    tps = seq // tm_prompt

    gw, nh = WIDTH, N_HEADS

    w_main = jnp.concatenate([w_in[:, :, :4 * gw], w_in[:, :, 4 * gw + 2 * nh:]], axis=2).astype(MXU_DTYPE)
    w_ba = jnp.pad(w_in[:, :, 4 * gw:4 * gw + 2 * nh], ((0, 0), (0, 0), (0, LANES - 2 * nh))).astype(MXU_DTYPE)
    w_out16, w_up16, w_down16 = (w.astype(MXU_DTYPE) for w in (w_out, w_up, w_down))
    alog_pad = jnp.pad(gdn_a_log, ((0, 0), (nh, LANES - 2 * nh)))[:, None, :]
    dtb_pad = jnp.pad(gdn_dt_bias, ((0, 0), (nh, LANES - 2 * nh)))[:, None, :]

    c_all = jnp.concatenate([c_prompt, c_sample], axis=0)

    def mods(w_ada, b_ada):
        mod = _adaln(c_all, w_ada, b_ada).reshape(depth, bsz + dbsz, 3, d).transpose(0, 2, 1, 3)
        return mod[:, :, :bsz, None, :], jnp.repeat(mod[:, :, bsz:, :], dseq, axis=2)[:, :, None]

    mix_p, mix_s = mods(w_ada_mix, b_ada_mix)
    ffn_p, ffn_s = mods(w_ada_ffn, b_ada_ffn)
    lng, lnb, flng, flnb, ng, fcb = (a[:, None, :] for a in (ln_mix_g, ln_mix_b, ln_ffn_g, ln_ffn_b,
                                                               gdn_norm_g, ffn_conv_b))

    rope_p = _rope_tables(jnp.arange(seq))
    rope_s = _rope_tables(jnp.tile(past_len + jnp.arange(dseq), dbsz))
    pt_flat = page_table.reshape(-1)

    xp = x_prompt.reshape(mp, d)
    xs = x_sample.reshape(ms, d)
    p_idx = lambda i: i // tps
    zero_idx = lambda i: 0
    outs = [[] for _ in range(10)]
    tps_in = seq // tm_in
    fprev1 = jnp.broadcast_to(state_ffn_conv[:, :, 1:2, :], (depth, dbsz, dseq, d_ff)).reshape(depth, ms, d_ff)
    fprev2 = jnp.concatenate([state_ffn_conv, jnp.zeros((depth, dbsz, dseq - 2, d_ff), F32)],
                             axis=2).reshape(depth, ms, d_ff)
    for l in range(depth):
        g, ba, mq, mk, mv = _in_proj(xp, mix_p, w_main, w_ba, rope_p, l, tm_in, tn_in,
                                     lambda i: i // tps_in, lambda i: i % tps_in)
        o_gdn, s_p = _gdn_prompt(g, ba, gdn_conv_w, alog_pad, dtb_pad, ng, l, bsz, seq, tt_gdn)
        o_moba = _moba_prompt(mq, mk, mv, bsz, seq)
        xp = _out_ln(o_gdn, o_moba, xp, mix_p, w_out16, lng, lnb, l, tm_prompt, p_idx, alpha)
        xp, fcp = _ffn(xp, ffn_p, w_up16, w_down16, ffn_conv_w, fcb, flng, flnb, l, tm_prompt, tc_ffn, p_idx, alpha,
                       tiles_per_seq=tps)
        fcp = fcp[tps - 1::tps]
        gcp = g.reshape(bsz, seq, 4 * WIDTH)[:, seq - (GDN_CONV - 1):, :3 * WIDTH]

        gs, bas, mqs, mks, mvs = _in_proj(xs, mix_s, w_main, w_ba, rope_s, l, ms, WIDTH, zero_idx, zero_idx)
        gs3 = gs.reshape(dbsz, dseq, 4 * WIDTH)
        o_gdn_s, s_s = _gdn_sample(gs3, bas.reshape(dbsz, dseq, LANES), state_gdn_conv, state_gdn,
                                   gdn_conv_w, alog_pad, dtb_pad, ng, l, dbsz, dseq)
        mq3, mk3, mv3 = (a.reshape(dbsz, dseq, WIDTH) for a in (mqs, mks, mvs))
        mq8, mk8, mv8 = (jnp.pad(a, ((0, 0), (0, QPAD - dseq), (0, 0))) for a in (mq3, mk3, mv3))
        mq_hm = mq8.reshape(dbsz, QPAD, nh, HEAD_DIM).transpose(0, 2, 1, 3).reshape(dbsz, nh * QPAD, HEAD_DIM)
        kmean, pm, pls, pacc = _moba_partials(pt_flat, mq_hm, cache_k, cache_v, l, dbsz, nb_past, n_pages)
        o_moba_s = _moba_merge(mq8, mk8, mv8, kmean, pm, pls, pacc, dbsz, dseq, nb_past)
        xs = _out_ln(o_gdn_s.reshape(ms, WIDTH), o_moba_s.reshape(ms, WIDTH), xs, mix_s, w_out16, lng, lnb,
                     l, ms, zero_idx, alpha)
        xs, gfull = _ffn(xs, ffn_s, w_up16, w_down16, ffn_conv_w, fcb, flng, flnb, l, ms, tc_ffn, zero_idx, alpha,
                         prev_rows=(fprev1, fprev2), dec_seq=dseq)
        assert dseq >= GDN_CONV - 1
        gcs = gs3[:, dseq - (GDN_CONV - 1):, :3 * WIDTH]
        fcs = gfull.reshape(dbsz, dseq, d_ff)[:, dseq - (FFN_CONV - 1):, :]

        for lst, val in zip(outs, (mk.reshape(bsz, seq, nh, HEAD_DIM), mv.reshape(bsz, seq, nh, HEAD_DIM),
                                   mk3.reshape(dbsz, dseq, nh, HEAD_DIM), mv3.reshape(dbsz, dseq, nh, HEAD_DIM),
                                   s_p, s_s, gcp, gcs, fcp, fcs)):
            lst.append(val)

    return (xp.reshape(bsz, seq, d), xs.reshape(dbsz, dseq, d)) + tuple(jnp.stack(o) for o in outs)


def kernel(x_prompt, x_sample, cache_k, cache_v, state_gdn, state_gdn_conv, state_ffn_conv, page_table, c_prompt, c_sample, w_ada_mix, b_ada_mix, w_in, gdn_conv_w, gdn_a_log, gdn_dt_bias, gdn_norm_g, w_out, ln_mix_g, ln_mix_b, w_ada_ffn, b_ada_ffn, w_up, ffn_conv_w, ffn_conv_b, w_down, ln_ffn_g, ln_ffn_b):
    return _forward(x_prompt, x_sample, cache_k, cache_v, state_gdn, state_gdn_conv, state_ffn_conv, page_table,
                    c_prompt, c_sample, w_ada_mix, b_ada_mix, w_in, gdn_conv_w, gdn_a_log, gdn_dt_bias, gdn_norm_g,
                    w_out, ln_mix_g, ln_mix_b, w_ada_ffn, b_ada_ffn, w_up, ffn_conv_w, ffn_conv_b, w_down,
                    ln_ffn_g, ln_ffn_b, past_len=8192, tm_prompt=512, tm_in=1024, tn_in=512, tc_ffn=512, tt_gdn=256)
```

```python
import functools
import math

import jax
import jax.numpy as jnp
from jax import lax
from jax.experimental import pallas as pl
from jax.experimental.pallas import tpu as pltpu

F32 = jnp.float32
MXU_DTYPE = jnp.bfloat16

HEAD_DIM = 128
LANES = 128
SUBLANES = 8
N_HEADS = 8
WIDTH = N_HEADS * HEAD_DIM
GDN_CONV = 4
FFN_CONV = 3
MOBA_BLOCK = 256
MOBA_TOPK = 3
PAGE_SIZE = 128
ROPE_THETA = 500000.0
ROPE_DIM = HEAD_DIM // 4
LN_EPS = 1e-5
NORM_EPS = 1e-6
GDN_CHUNK = 128
INV_BLOCK = 16
VMEM_LIMIT = 56 * 1024 * 1024
ROW_SUBTILE = 256
NEG_INF = float("-inf")


def _cparams(*sem):
    return pltpu.CompilerParams(dimension_semantics=sem, vmem_limit_bytes=VMEM_LIMIT)


def _mm(a, b):
    return jnp.dot(a.astype(MXU_DTYPE), b.astype(MXU_DTYPE), preferred_element_type=F32)


def _mm_nt(a, b):
    return lax.dot_general(a.astype(MXU_DTYPE), b.astype(MXU_DTYPE), (((1,), (1,)), ((), ())),
                           preferred_element_type=F32)


def _mm_tn(a, b):
    return lax.dot_general(a.astype(MXU_DTYPE), b.astype(MXU_DTYPE), (((0,), (0,)), ((), ())),
                           preferred_element_type=F32)


def _silu(x):
    return x * jax.nn.sigmoid(x)


def _layer_norm_rows(r, g, b):
    mu = jnp.mean(r, axis=-1, keepdims=True)
    d = r - mu
    var = jnp.mean(d * d, axis=-1, keepdims=True)
    return d * lax.rsqrt(var + LN_EPS) * g + b


def _adaln_kernel(c_ref, w_ref, b_ref, o_ref):
    o_ref[...] = _mm(c_ref[...], w_ref[...]) + b_ref[...]


def _adaln(c, w, b):
    n_layers, d, n = w.shape
    rows = c.shape[0]
    tn = 2048
    return pl.pallas_call(
        _adaln_kernel,
        out_shape=jax.ShapeDtypeStruct((n_layers, rows, n), F32),
        grid=(n_layers, n // tn),
        in_specs=[pl.BlockSpec((rows, d), lambda l, j: (0, 0)),
                  pl.BlockSpec((None, d, tn), lambda l, j: (l, 0, j)),
                  pl.BlockSpec((None, 1, tn), lambda l, j: (l, 0, j))],
        out_specs=pl.BlockSpec((None, rows, tn), lambda l, j: (l, 0, j)),
        compiler_params=_cparams("arbitrary", "arbitrary"),
        name="adaln",
    )(c, w, b.reshape(n_layers, 1, n))


def _rope_proj_store(h_ref, w_ref, rc_ref, rs1_ref, rs2_ref, o_ref):
    half = ROPE_DIM // 2
    tm = h_ref.shape[0]
    rs = min(tm, ROW_SUBTILE)
    subs = [slice(s * rs, (s + 1) * rs) for s in range(tm // rs)]
    ys = [jnp.dot(h_ref[sl, :], w_ref[...], preferred_element_type=F32) for sl in subs]
    for sl, y in zip(subs, ys):
        c, s1, s2 = rc_ref[sl, :], rs1_ref[sl, :], rs2_ref[sl, :]
        for h in range(y.shape[1] // HEAD_DIM):
            yh = y[:, h * HEAD_DIM:(h + 1) * HEAD_DIM]
            o_ref[sl, h * HEAD_DIM:(h + 1) * HEAD_DIM] = (
                yh * c + pltpu.roll(yh, HEAD_DIM - half, 1) * s1 + pltpu.roll(yh, half, 1) * s2)


def _in_proj_kernel(x_ref, sc_ref, sh_ref, w_ref, wba_ref, rc_ref, rs1_ref, rs2_ref,
                    g_ref, ba_ref, mq_ref, mk_ref, mv_ref, h_ref, *, n_g, n_m):
    j = pl.program_id(1)

    @pl.when(j == 0)
    def _():
        h_ref[...] = (x_ref[...] * (1.0 + sc_ref[...]) + sh_ref[...]).astype(h_ref.dtype)
        ba_ref[...] = jnp.dot(h_ref[...], wba_ref[...], preferred_element_type=F32)

    def proj():
        return jnp.dot(h_ref[...], w_ref[...], preferred_element_type=F32)

    @pl.when(j < n_g)
    def _():
        g_ref[...] = proj()

    @pl.when((j >= n_g) & (j < n_g + n_m))
    def _():
        _rope_proj_store(h_ref, w_ref, rc_ref, rs1_ref, rs2_ref, mq_ref)

    @pl.when((j >= n_g + n_m) & (j < n_g + 2 * n_m))
    def _():
        _rope_proj_store(h_ref, w_ref, rc_ref, rs1_ref, rs2_ref, mk_ref)

    @pl.when(j >= n_g + 2 * n_m)
    def _():
        mv_ref[...] = proj()


def _mod_spec(mod, layer, part, index, n_grid):
    r, d = mod.shape[3], mod.shape[4]
    if n_grid == 1:
        return pl.BlockSpec((None, None, None, r, d), lambda i: (layer, part, index(i), 0, 0))
    return pl.BlockSpec((None, None, None, r, d), lambda i, j: (layer, part, index(i), 0, 0))


def _in_proj(x, mod, w_all, w_ba, rope, layer, tm, tn, mod_index, rope_index):
    m, d = x.shape
    n_g, n_m = 4 * WIDTH // tn, WIDTH // tn
    rope_spec = pl.BlockSpec((tm, LANES), lambda i, j: (rope_index(i), 0))

    def m_spec(first):
        return pl.BlockSpec((tm, tn), lambda i, j: (i, jnp.clip(j - first, 0, n_m - 1)))

    return pl.pallas_call(
        functools.partial(_in_proj_kernel, n_g=n_g, n_m=n_m),
        out_shape=(jax.ShapeDtypeStruct((m, 4 * WIDTH), F32),
                   jax.ShapeDtypeStruct((m, LANES), F32),
                   jax.ShapeDtypeStruct((m, WIDTH), F32),
                   jax.ShapeDtypeStruct((m, WIDTH), F32),
                   jax.ShapeDtypeStruct((m, WIDTH), F32)),
        grid=(m // tm, n_g + 3 * n_m),
        in_specs=[pl.BlockSpec((tm, d), lambda i, j: (i, 0)),
                  _mod_spec(mod, layer, 1, mod_index, 2), _mod_spec(mod, layer, 0, mod_index, 2),
                  pl.BlockSpec((None, d, tn), lambda i, j: (layer, 0, j)),
                  pl.BlockSpec((None, d, LANES), lambda i, j: (layer, 0, 0)),
                  rope_spec, rope_spec, rope_spec],
        out_specs=(pl.BlockSpec((tm, tn), lambda i, j: (i, jnp.minimum(j, n_g - 1))),
                   pl.BlockSpec((tm, LANES), lambda i, j: (i, 0)),
                   m_spec(n_g), m_spec(n_g + n_m), m_spec(n_g + 2 * n_m)),
        scratch_shapes=[pltpu.VMEM((tm, d), MXU_DTYPE)],
        compiler_params=_cparams("arbitrary", "arbitrary"),
        name="in_proj",
    )(x, mod, mod, w_all, w_ba, *rope)


def _unit_lower_inverse(a_low, eye, same_blk):
    items = range(len(a_low))
    d = [jnp.where(same_blk, -a, 0.0) for a in a_low]
    o = [-a_low[i] - d[i] for i in items]
    x = [eye + d[i] for i in items]
    p = d
    for _ in range(int(math.log2(INV_BLOCK)) - 1):
        p = [_mm(p[i], p[i]) for i in items]
        x = [x[i] + _mm(x[i], p[i]) for i in items]
    y = [_mm(x[i], o[i]) for i in items]
    z = [eye + y[i] for i in items]
    p = y
    for _ in range(int(math.log2(GDN_CHUNK // INV_BLOCK)) - 1):
        p = [_mm(p[i], p[i]) for i in items]
        z = [z[i] + _mm(z[i], p[i]) for i in items]
    return z, x


def _gdn_wy(q, k, v, beta_c, eg_c, gc_col, gc_row, masks):
    eye, same_blk, tri_incl, tri_strict = masks
    items = range(len(q))
    c = q[0].shape[0]
    kb = [k[i] * beta_c[i] for i in items]
    p = [_mm_nt(jnp.concatenate([q[i], kb[i]], axis=0), k[i]) for i in items]
    decay = [jnp.exp(jnp.where(tri_incl, gc_col[i] - gc_row[i], NEG_INF)) for i in items]
    qk = [p[i][:c] * decay[i] for i in items]
    a_low = [jnp.where(tri_strict, p[i][c:] * decay[i], 0.0) for i in items]
    z, x = _unit_lower_inverse(a_low, eye, same_blk)
    rhs = [jnp.concatenate([v[i] * beta_c[i], kb[i] * eg_c[i]], axis=1) for i in items]
    t = [_mm(x[i], rhs[i]) for i in items]
    sol = [_mm(z[i], t[i]) for i in items]
    return [s[:, :HEAD_DIM] for s in sol], [s[:, HEAD_DIM:] for s in sol], qk


def _gdn_state_step(q, k, u, w, qk, eg_c, ek_c, gl, s):
    items = range(len(q))
    v_new = [u[i] - _mm(w[i], s[i]) for i in items]
    o = [_mm(q[i] * eg_c[i], s[i]) + _mm(qk[i], v_new[i]) for i in items]
    s_new = [s[i] * gl[i] + _mm_tn(k[i] * ek_c[i], v_new[i]) for i in items]
    return o, s_new


def _gdn_kernel(x_ref, ba_ref, cw_ref, alog_ref, dtb_ref, ng_ref, *rest, tt, n_valid, from_state):
    if from_state:
        cs_ref, s0_ref, o_ref, sout_ref, xbuf, zbuf, babuf, s_scr = rest
    else:
        o_ref, sout_ref, xbuf, s_scr = rest
    t = pl.program_id(1)
    nk = GDN_CONV - 1
    c = GDN_CHUNK

    if from_state:
        xbuf[...] = jnp.zeros(xbuf.shape, F32)
        xbuf[SUBLANES - nk:SUBLANES, :] = cs_ref[...]
        xbuf[SUBLANES:SUBLANES + n_valid, :] = x_ref[:, 0:3 * WIDTH]
        zbuf[...] = jnp.zeros(zbuf.shape, F32)
        zbuf[0:n_valid, :] = x_ref[:, 3 * WIDTH:4 * WIDTH]
        babuf[...] = jnp.zeros(babuf.shape, F32)
        babuf[0:n_valid, :] = ba_ref[...]
        s_scr[...] = s0_ref[...]
        ba = babuf[...]
    else:
        @pl.when(t == 0)
        def _():
            xbuf[0:SUBLANES, :] = jnp.zeros((SUBLANES, 3 * WIDTH), F32)
            s_scr[...] = jnp.zeros(s_scr.shape, F32)

        @pl.when(t > 0)
        def _():
            xbuf[0:SUBLANES, :] = xbuf[tt:tt + SUBLANES, :]

        xbuf[SUBLANES:SUBLANES + tt, :] = x_ref[:, 0:3 * WIDTH]
        ba = ba_ref[...]

    beta_all = jax.nn.sigmoid(ba)
    a_in = ba + dtb_ref[...]
    softplus = jnp.maximum(a_in, 0.0) + jnp.log1p(jnp.exp(-jnp.abs(a_in)))
    g_all = -jnp.exp(alog_ref[...]) * softplus
    if n_valid < tt:
        live = lax.broadcasted_iota(jnp.int32, (tt, LANES), 0) < n_valid
        beta_all = jnp.where(live, beta_all, 0.0)
        g_all = jnp.where(live, g_all, 0.0)

    row = lax.broadcasted_iota(jnp.int32, (c, c), 0)
    col = lax.broadcasted_iota(jnp.int32, (c, c), 1)
    tri_incl = row >= col
    tri_strict = row > col
    masks = (jnp.where(row == col, 1.0, 0.0).astype(F32), (row // INV_BLOCK) == (col // INV_BLOCK),
             tri_incl, tri_strict)
    ltri = jnp.where(tri_incl, 1.0, 0.0).astype(F32)

    def conv_act(col0):
        acc = xbuf[SUBLANES - nk:SUBLANES - nk + tt, col0:col0 + HEAD_DIM] * cw_ref[0:1, col0:col0 + HEAD_DIM]
        for i in range(1, GDN_CONV):
            acc = acc + (xbuf[SUBLANES - nk + i:SUBLANES - nk + i + tt, col0:col0 + HEAD_DIM]
                         * cw_ref[i:i + 1, col0:col0 + HEAD_DIM])
        return _silu(acc)

    def l2n(a):
        return a * lax.rsqrt(jnp.sum(a * a, axis=-1, keepdims=True) + NORM_EPS)

    chunk_decay = []
    for ci in range(tt // c):
        gcum = jnp.dot(ltri, g_all[ci * c:(ci + 1) * c, :], precision=lax.Precision.HIGHEST,
                       preferred_element_type=F32)
        chunk_decay.append((gcum, gcum.T, jnp.exp(gcum), jnp.exp(gcum[c - 1:c, :] - gcum)))

    heads = range(N_HEADS)
    n_chunks = tt // c
    q_t = [l2n(conv_act(h * HEAD_DIM)) * (HEAD_DIM ** -0.5) for h in heads]
    k_t = [l2n(conv_act(WIDTH + h * HEAD_DIM)) for h in heads]
    v_t = [conv_act(2 * WIDTH + h * HEAD_DIM) for h in heads]

    pairs = [(ci, h) for ci in range(n_chunks) for h in heads]
    rows = lambda a, ci: a[ci * c:(ci + 1) * c]
    lane = lambda a, h: a[:, N_HEADS + h:N_HEADS + h + 1]
    qs = [rows(q_t[h], ci) for ci, h in pairs]
    ks = [rows(k_t[h], ci) for ci, h in pairs]
    egs = [lane(chunk_decay[ci][2], h) for ci, h in pairs]
    u, w, qk = _gdn_wy(
        qs, ks, [rows(v_t[h], ci) for ci, h in pairs],
        [rows(beta_all, ci)[:, h:h + 1] for ci, h in pairs], egs,
        [lane(chunk_decay[ci][0], h) for ci, h in pairs],
        [chunk_decay[ci][1][N_HEADS + h:N_HEADS + h + 1, :] for ci, h in pairs], masks)

    s = [s_scr[h] for h in heads]
    for ci in range(n_chunks):
        sl = slice(ci * N_HEADS, (ci + 1) * N_HEADS)
        eks = [lane(chunk_decay[ci][3], h) for h in heads]
        gls = [lane(chunk_decay[ci][2], h)[c - 1:c] for h in heads]
        o, s = _gdn_state_step(qs[sl], ks[sl], u[sl], w[sl], qk[sl], egs[sl], eks, gls, s)
        r0 = ci * c
        for h in heads:
            hs = slice(h * HEAD_DIM, (h + 1) * HEAD_DIM)
            if from_state:
                zh = zbuf[r0:r0 + c, hs]
            else:
                zh = x_ref[r0:r0 + c, 3 * WIDTH + h * HEAD_DIM:3 * WIDTH + (h + 1) * HEAD_DIM]
            on = o[h] * lax.rsqrt(jnp.mean(o[h] * o[h], axis=-1, keepdims=True) + NORM_EPS) * ng_ref[...] * _silu(zh)
            if from_state:
                o_ref[:, hs] = on[0:n_valid].astype(o_ref.dtype)
            else:
                o_ref[r0:r0 + c, hs] = on.astype(o_ref.dtype)
    for h in heads:
        s_scr[h] = s[h]
        sout_ref[h] = s[h]


def _gdn_prompt(g, ba, conv_w, alog_pad, dtb_pad, norm_g, layer, batch, seq, tt):
    m = g.shape[0]
    nt = seq // tt
    kern = functools.partial(_gdn_kernel, tt=tt, n_valid=tt, from_state=False)
    small = lambda shape: pl.BlockSpec((None,) + shape, lambda b, t: (layer, 0, 0))
    return pl.pallas_call(
        kern,
        out_shape=(jax.ShapeDtypeStruct((m, WIDTH), MXU_DTYPE),
                   jax.ShapeDtypeStruct((batch, N_HEADS, HEAD_DIM, HEAD_DIM), F32)),
        grid=(batch, nt),
        in_specs=[pl.BlockSpec((tt, 4 * WIDTH), lambda b, t: (b * nt + t, 0)),
                  pl.BlockSpec((tt, LANES), lambda b, t: (b * nt + t, 0)),
                  small((GDN_CONV, 3 * WIDTH)), small((1, LANES)), small((1, LANES)), small((1, HEAD_DIM))],
        out_specs=(pl.BlockSpec((tt, WIDTH), lambda b, t: (b * nt + t, 0)),
                   pl.BlockSpec((None, N_HEADS, HEAD_DIM, HEAD_DIM), lambda b, t: (b, 0, 0, 0))),
        scratch_shapes=[pltpu.VMEM((tt + SUBLANES, 3 * WIDTH), F32),
                        pltpu.VMEM((N_HEADS, HEAD_DIM, HEAD_DIM), F32)],
        compiler_params=_cparams("arbitrary", "arbitrary"),
        name="gdn_prompt",
    )(g, ba, conv_w, alog_pad, dtb_pad, norm_g)


def _gdn_sample(g, ba, conv_state, s0, conv_w, alog_pad, dtb_pad, norm_g, layer, batch, dec_seq):
    tt = GDN_CHUNK
    kern = functools.partial(_gdn_kernel, tt=tt, n_valid=dec_seq, from_state=True)
    small = lambda shape: pl.BlockSpec((None,) + shape, lambda b, t: (layer, 0, 0))
    return pl.pallas_call(
        kern,
        out_shape=(jax.ShapeDtypeStruct((batch, dec_seq, WIDTH), F32),
                   jax.ShapeDtypeStruct((batch, N_HEADS, HEAD_DIM, HEAD_DIM), F32)),
        grid=(batch, 1),
        in_specs=[pl.BlockSpec((None, dec_seq, 4 * WIDTH), lambda b, t: (b, 0, 0)),
                  pl.BlockSpec((None, dec_seq, LANES), lambda b, t: (b, 0, 0)),
                  small((GDN_CONV, 3 * WIDTH)), small((1, LANES)), small((1, LANES)), small((1, HEAD_DIM)),
                  pl.BlockSpec((None, None, GDN_CONV - 1, 3 * WIDTH), lambda b, t: (layer, b, 0, 0)),
                  pl.BlockSpec((None, None, N_HEADS, HEAD_DIM, HEAD_DIM), lambda b, t: (layer, b, 0, 0, 0))],
        out_specs=(pl.BlockSpec((None, dec_seq, WIDTH), lambda b, t: (b, 0, 0)),
                   pl.BlockSpec((None, N_HEADS, HEAD_DIM, HEAD_DIM), lambda b, t: (b, 0, 0, 0))),
        scratch_shapes=[pltpu.VMEM((tt + SUBLANES, 3 * WIDTH), F32),
                        pltpu.VMEM((tt, WIDTH), F32),
                        pltpu.VMEM((tt, LANES), F32),
                        pltpu.VMEM((N_HEADS, HEAD_DIM, HEAD_DIM), F32)],
        compiler_params=_cparams("arbitrary", "arbitrary"),
        name="gdn_sample",
    )(g, ba, conv_w, alog_pad, dtb_pad, norm_g, conv_state, s0)


def _topk_select(gate, n_valid, idx, axis):
    n = gate.shape[axis]
    cnt = jnp.zeros(gate.shape, F32)
    for j in range(n):
        cand = lax.slice_in_dim(gate, j, j + 1, axis=axis)
        beats = jnp.where(cand > gate, 1.0, jnp.where((cand == gate) & (j < idx), 1.0, 0.0))
        cnt = cnt + jnp.where(j < n_valid, beats, 0.0)
    return (cnt < MOBA_TOPK) & (idx < n_valid)


MOBA_HEADS_PER_STEP = 4


def _moba_prompt_kernel(q_ref, k_ref, v_ref, o_ref, k16, vt16, kmean, *, nb):
    i = pl.program_id(2)
    blk = MOBA_BLOCK
    scale = HEAD_DIM ** -0.5
    group = range(MOBA_HEADS_PER_STEP)
    lanes = lambda g: slice(g * HEAD_DIM, (g + 1) * HEAD_DIM)

    @pl.when(i == 0)
    def _():
        for g in group:
            for j in range(nb):
                kj = k_ref[j * blk:(j + 1) * blk, lanes(g)]
                k16[g, j] = kj.astype(MXU_DTYPE)
                kmean[g, j:j + 1, :] = jnp.mean(kj, axis=0, keepdims=True)
                vt16[g, j] = v_ref[j * blk:(j + 1) * blk, lanes(g)].T.astype(MXU_DTYPE)

    kpos = lax.broadcasted_iota(jnp.int32, (blk, blk), 0)
    qpos = lax.broadcasted_iota(jnp.int32, (blk, blk), 1)

    def attend(cur):
        qt = [q_ref[:, lanes(g)].T for g in group]
        qt16 = [a.astype(MXU_DTYPE) for a in qt]
        if cur > MOBA_TOPK:
            gate = [jnp.dot(kmean[g], qt[g], precision=lax.Precision.HIGHEST, preferred_element_type=F32)
                    for g in group]
            blk_idx = lax.broadcasted_iota(jnp.int32, gate[0].shape, 0)
            sel = [jnp.where(_topk_select(gate[g], cur, blk_idx, 0), 1.0, 0.0) for g in group]
        s = [[] for _ in group]
        for j in range(cur + 1):
            for g in group:
                sj = jnp.dot(k16[g, j], qt16[g], preferred_element_type=F32) * scale
                if j == cur:
                    sj = jnp.where(kpos <= qpos, sj, NEG_INF)
                elif cur > MOBA_TOPK:
                    sj = jnp.where(sel[g][j:j + 1, :] > 0.5, sj, NEG_INF)
                s[g].append(sj)
        m = [functools.reduce(jnp.maximum, [jnp.max(sj, axis=0, keepdims=True) for sj in s[g]]) for g in group]
        l = [jnp.zeros((1, blk), F32) for _ in group]
        acc = [jnp.zeros((HEAD_DIM, blk), F32) for _ in group]
        for j in range(cur + 1):
            for g in group:
                p = jnp.exp(s[g][j] - m[g])
                l[g] = l[g] + jnp.sum(p, axis=0, keepdims=True)
                acc[g] = acc[g] + jnp.dot(vt16[g, j], p.astype(MXU_DTYPE), preferred_element_type=F32)
        for g in group:
            o_ref[:, lanes(g)] = (acc[g] / l[g]).T.astype(o_ref.dtype)

    for cur in range(nb):
        pl.when(i == cur)(functools.partial(attend, cur))


def _moba_prompt(mq, mk, mv, batch, seq):
    m = mq.shape[0]
    nb = seq // MOBA_BLOCK
    hps = MOBA_HEADS_PER_STEP
    kern = functools.partial(_moba_prompt_kernel, nb=nb)
    kv_spec = pl.BlockSpec((seq, hps * HEAD_DIM), lambda b, h, i: (b, h))
    q_spec = pl.BlockSpec((MOBA_BLOCK, hps * HEAD_DIM), lambda b, h, i: (b * nb + i, h))
    return pl.pallas_call(
        kern,
        out_shape=jax.ShapeDtypeStruct((m, WIDTH), MXU_DTYPE),
        grid=(batch, N_HEADS // hps, nb),
        in_specs=[q_spec, kv_spec, kv_spec],
        out_specs=q_spec,
        scratch_shapes=[pltpu.VMEM((hps, nb, MOBA_BLOCK, HEAD_DIM), MXU_DTYPE),
                        pltpu.VMEM((hps, nb, HEAD_DIM, MOBA_BLOCK), MXU_DTYPE),
                        pltpu.VMEM((hps, nb, HEAD_DIM), F32)],
        compiler_params=_cparams("arbitrary", "arbitrary", "arbitrary"),
        name="moba_prompt",
    )(mq, mk, mv)


QPAD = SUBLANES


PAST_BLOCKS_PER_STEP = 8


def _moba_partial_kernel(pt_ref, q_ref, *refs):
    nblk, ppb = PAST_BLOCKS_PER_STEP, MOBA_BLOCK // PAGE_SIZE
    k_refs, v_refs = refs[:nblk * ppb], refs[nblk * ppb:2 * nblk * ppb]
    kmean_ref, pm_ref, pl_ref, pacc_ref = refs[2 * nblk * ppb:]
    n = pl.program_id(1)
    scale = HEAD_DIM ** -0.5
    rows = PAGE_SIZE * N_HEADS
    q16 = q_ref[...].astype(MXU_DTYPE)
    q_head = lax.broadcasted_iota(jnp.int32, (N_HEADS * QPAD, rows), 0) // QPAD
    k_head = lax.broadcasted_iota(jnp.int32, (N_HEADS * QPAD, rows), 1) % N_HEADS
    same_head = q_head == k_head
    for g in range(nblk):
        ks = [k_refs[g * ppb + pg][...] for pg in range(ppb)]
        kmean_ref[n * nblk + g] = sum(jnp.sum(k, axis=0) for k in ks) * (1.0 / MOBA_BLOCK)
        s = [jnp.where(same_head, _mm_nt(q16, k.reshape(rows, HEAD_DIM)) * scale, NEG_INF) for k in ks]
        mx = functools.reduce(jnp.maximum, [jnp.max(sp, axis=-1, keepdims=True) for sp in s])
        p = [jnp.exp(sp - mx) for sp in s]
        lsum = sum(jnp.sum(pp, axis=-1, keepdims=True) for pp in p)
        pm_ref[g] = jnp.broadcast_to(mx, pm_ref.shape[1:])
        pl_ref[g] = jnp.broadcast_to(lsum, pl_ref.shape[1:])
        pacc_ref[g] = sum(_mm(p[pg], v_refs[g * ppb + pg][...].reshape(rows, HEAD_DIM)) for pg in range(ppb))


def _moba_partials(page_table_flat, q, cache_k, cache_v, layer, batch, nb, n_pages):
    nblk, ppb = PAST_BLOCKS_PER_STEP, MOBA_BLOCK // PAGE_SIZE
    assert nb % nblk == 0
    hq = N_HEADS * QPAD

    def page_spec(g, pg):
        return pl.BlockSpec((None, None, PAGE_SIZE, N_HEADS, HEAD_DIM),
                            lambda b, n, pt: (layer, pt[b * n_pages + ppb * (n * nblk + g) + pg], 0, 0, 0))

    page_specs = [page_spec(g, pg) for g in range(nblk) for pg in range(ppb)]
    part_shape = jax.ShapeDtypeStruct((batch, nb, hq, LANES), F32)
    part_spec = pl.BlockSpec((None, nblk, hq, LANES), lambda b, n, pt: (b, n, 0, 0))
    return pl.pallas_call(
        _moba_partial_kernel,
        out_shape=(jax.ShapeDtypeStruct((batch, nb, N_HEADS, HEAD_DIM), F32), part_shape, part_shape, part_shape),
        grid_spec=pltpu.PrefetchScalarGridSpec(
            num_scalar_prefetch=1,
            grid=(batch, nb // nblk),
            in_specs=[pl.BlockSpec((None, hq, HEAD_DIM), lambda b, n, pt: (b, 0, 0))] + page_specs + page_specs,
            out_specs=(pl.BlockSpec((None, nb, N_HEADS, HEAD_DIM), lambda b, n, pt: (b, 0, 0, 0)),
                       part_spec, part_spec, part_spec)),
        compiler_params=_cparams("arbitrary", "arbitrary"),
        name="moba_sample_partials",
    )(page_table_flat, q, *([cache_k] * (nblk * ppb)), *([cache_v] * (nblk * ppb)))


def _moba_merge_kernel(q_ref, k_ref, v_ref, kmean_ref, pm_ref, pl_ref, pacc_ref, o_ref, *, dec_seq, nb):
    scale = HEAD_DIM ** -0.5
    q8, k8, v8 = q_ref[...], k_ref[...], v_ref[...]
    qi = lax.broadcasted_iota(jnp.int32, (QPAD, QPAD), 0)
    kj = lax.broadcasted_iota(jnp.int32, (QPAD, QPAD), 1)
    own_ok = (kj <= qi) & (kj < dec_seq)
    blk_idx = lax.broadcasted_iota(jnp.int32, (QPAD, nb), 1)
    for h in range(N_HEADS):
        hs = slice(h * HEAD_DIM, (h + 1) * HEAD_DIM)
        hq = slice(h * QPAD, (h + 1) * QPAD)
        gate = lax.dot_general(q8[:, hs], kmean_ref[:, h, :], (((1,), (1,)), ((), ())),
                               precision=lax.Precision.HIGHEST, preferred_element_type=F32)
        sel = jnp.where(_topk_select(gate, nb, blk_idx, 1), 1.0, 0.0)
        s_own = jnp.where(own_ok, _mm_nt(q8[:, hs], k8[:, hs]) * scale, NEG_INF)
        m_tot = jnp.broadcast_to(jnp.max(s_own, axis=-1, keepdims=True), (QPAD, LANES))
        for n in range(nb):
            m_tot = jnp.maximum(m_tot, jnp.where(sel[:, n:n + 1] > 0.5, pm_ref[n, hq, :], NEG_INF))
        p_own = jnp.exp(s_own - m_tot[:, 0:1])
        l_tot = jnp.broadcast_to(jnp.sum(p_own, axis=-1, keepdims=True), (QPAD, LANES))
        acc = jnp.zeros((QPAD, HEAD_DIM), F32)
        for j in range(dec_seq):
            acc = acc + p_own[:, j:j + 1] * v8[j:j + 1, hs]
        for n in range(nb):
            w = jnp.where(sel[:, n:n + 1] > 0.5, jnp.exp(pm_ref[n, hq, :] - m_tot), 0.0)
            l_tot = l_tot + w * pl_ref[n, hq, :]
            acc = acc + w * pacc_ref[n, hq, :]
        o_ref[:, hs] = (acc / l_tot)[0:dec_seq].astype(o_ref.dtype)


def _moba_merge(q, k, v, kmean, pm, plsum, pacc, batch, dec_seq, nb):
    kern = functools.partial(_moba_merge_kernel, dec_seq=dec_seq, nb=nb)
    row_spec = pl.BlockSpec((None, QPAD, WIDTH), lambda b: (b, 0, 0))
    part_spec = pl.BlockSpec((None, nb, N_HEADS * QPAD, LANES), lambda b: (b, 0, 0, 0))
    return pl.pallas_call(
        kern,
        out_shape=jax.ShapeDtypeStruct((batch, dec_seq, WIDTH), F32),
        grid=(batch,),
        in_specs=[row_spec, row_spec, row_spec,
                  pl.BlockSpec((None, nb, N_HEADS, HEAD_DIM), lambda b: (b, 0, 0, 0)),
                  part_spec, part_spec, part_spec],
        out_specs=pl.BlockSpec((None, dec_seq, WIDTH), lambda b: (b, 0, 0)),
        compiler_params=_cparams("arbitrary"),
        name="moba_sample_merge",
    )(q, k, v, kmean, pm, plsum, pacc)


def _out_ln_kernel(og_ref, om_ref, x_ref, gate_ref, w1_ref, w2_ref, lng_ref, lnb_ref, o_ref, *, alpha):
    tm = x_ref.shape[0]
    rs = min(tm, ROW_SUBTILE)
    subs = [slice(s * rs, (s + 1) * rs) for s in range(tm // rs)]
    y = [_mm(og_ref[sl, :], w1_ref[...]) + _mm(om_ref[sl, :], w2_ref[...]) for sl in subs]
    for sl, ys in zip(subs, y):
        gate = gate_ref[...] if gate_ref.shape[0] == 1 else gate_ref[sl, :]
        r = alpha * x_ref[sl, :] + (1.0 + gate) * ys
        o_ref[sl, :] = _layer_norm_rows(r, lng_ref[...], lnb_ref[...])


def _out_ln(o_gdn, o_moba, x, mod, w_out, ln_g, ln_b, layer, tm, mod_index, alpha):
    m, d = x.shape
    vec = pl.BlockSpec((None, 1, d), lambda i: (layer, 0, 0))
    return pl.pallas_call(
        functools.partial(_out_ln_kernel, alpha=alpha),
        out_shape=jax.ShapeDtypeStruct((m, d), F32),
        grid=(m // tm,),
        in_specs=[pl.BlockSpec((tm, WIDTH), lambda i: (i, 0)), pl.BlockSpec((tm, WIDTH), lambda i: (i, 0)),
                  pl.BlockSpec((tm, d), lambda i: (i, 0)),
                  _mod_spec(mod, layer, 2, mod_index, 1),
                  pl.BlockSpec((None, WIDTH, d), lambda i: (layer, 0, 0)),
                  pl.BlockSpec((None, WIDTH, d), lambda i: (layer, 1, 0)),
                  vec, vec],
        out_specs=pl.BlockSpec((tm, d), lambda i: (i, 0)),
        compiler_params=_cparams("arbitrary"),
        name="out_proj_ln",
    )(o_gdn, o_moba, x, mod, w_out, w_out, ln_g, ln_b)


def _ffn_kernel(x_ref, sc_ref, sh_ref, gate_ref, wg_ref, wu_ref, wd_ref, cw_ref, cb_ref, lng_ref, lnb_ref, *rest,
                tm, n_sub, n_chunks, tiles_per_seq, dec_seq, alpha):
    sample = dec_seq is not None
    if sample:
        p1_ref, p2_ref, o_ref, st_ref, h_ref, acc_ref, gbuf = rest
    else:
        o_ref, st_ref, h_ref, acc_ref, gbuf, carry = rest
    i = pl.program_id(0)
    c = pl.program_id(1)
    tc = wg_ref.shape[1]
    rs = tm // n_sub
    subs = [slice(s * rs, (s + 1) * rs) for s in range(n_sub)]
    mod_rows = lambda ref, sl: ref[...] if ref.shape[0] == 1 else ref[sl, :]

    def chunk_step(first_chunk, last_chunk):
        if sample:
            gbuf[0:SUBLANES, :] = jnp.zeros((SUBLANES, tc), F32)
        else:
            seq_start = (i % tiles_per_seq) == 0

            @pl.when(seq_start)
            def _():
                gbuf[0:SUBLANES, :] = jnp.zeros((SUBLANES, tc), F32)

            @pl.when(jnp.logical_not(seq_start))
            def _():
                gbuf[0:SUBLANES, :] = carry[c]

        g, u = [], []
        for s, sl in enumerate(subs):
            if first_chunk:
                h_ref[sl, :] = (x_ref[sl, :] * (1.0 + mod_rows(sc_ref, sl)) + mod_rows(sh_ref, sl)).astype(h_ref.dtype)
            g.append(jnp.dot(h_ref[sl, :], wg_ref[...], preferred_element_type=F32))
            gbuf[SUBLANES + s * rs:SUBLANES + (s + 1) * rs, :] = g[s]
            u.append(jnp.dot(h_ref[sl, :], wu_ref[...], preferred_element_type=F32))
        if not sample:
            carry[c] = g[-1][rs - SUBLANES:rs, :]
            st_ref[...] = g[-1][rs - (FFN_CONV - 1):rs, :]
        for s, sl in enumerate(subs):
            prev1 = gbuf[SUBLANES - 1 + s * rs:SUBLANES - 1 + (s + 1) * rs, :]
            prev2 = gbuf[SUBLANES - 2 + s * rs:SUBLANES - 2 + (s + 1) * rs, :]
            if sample:
                pos = lax.broadcasted_iota(jnp.int32, (rs, tc), 0) % dec_seq
                prev1 = jnp.where(pos >= 1, prev1, p1_ref[sl, :])
                prev2 = jnp.where(pos >= 2, prev2, p2_ref[sl, :])
                st_ref[sl, :] = g[s]
            conv = prev2 * cw_ref[0:1, :] + prev1 * cw_ref[1:2, :] + g[s] * cw_ref[2:3, :]
            act = _silu(conv + cb_ref[...]) * u[s]
            down = jnp.dot(act.astype(MXU_DTYPE), wd_ref[...], preferred_element_type=F32)
            total = down if first_chunk else acc_ref[sl, :] + down
            if last_chunk:
                r = alpha * x_ref[sl, :] + (1.0 + mod_rows(gate_ref, sl)) * total
                o_ref[sl, :] = _layer_norm_rows(r, lng_ref[...], lnb_ref[...])
            else:
                acc_ref[sl, :] = total

    assert n_chunks >= 2
    pl.when(c == 0)(functools.partial(chunk_step, True, False))
    pl.when((c > 0) & (c < n_chunks - 1))(functools.partial(chunk_step, False, False))
    pl.when(c == n_chunks - 1)(functools.partial(chunk_step, False, True))


def _ffn(x, mod, w_up, w_down, conv_w, conv_b, ln_g, ln_b, layer, tm, tc, mod_index, alpha,
         tiles_per_seq=None, prev_rows=None, dec_seq=None):
    m, d = x.shape
    d_ff = w_down.shape[1]
    n_chunks = d_ff // tc
    sample = prev_rows is not None
    vec = pl.BlockSpec((None, 1, d), lambda i, c: (layer, 0, 0))
    in_specs = [pl.BlockSpec((tm, d), lambda i, c: (i, 0)),
                _mod_spec(mod, layer, 1, mod_index, 2), _mod_spec(mod, layer, 0, mod_index, 2),
                _mod_spec(mod, layer, 2, mod_index, 2),
                pl.BlockSpec((None, d, tc), lambda i, c: (layer, 0, c)),
                pl.BlockSpec((None, d, tc), lambda i, c: (layer, 0, n_chunks + c)),
                pl.BlockSpec((None, tc, d), lambda i, c: (layer, c, 0)),
                pl.BlockSpec((None, FFN_CONV, tc), lambda i, c: (layer, 0, c)),
                pl.BlockSpec((None, 1, tc), lambda i, c: (layer, 0, c)),
                vec, vec]
    args = [x, mod, mod, mod, w_up, w_up, w_down, conv_w, conv_b, ln_g, ln_b]
    scratch = [pltpu.VMEM((tm, d), MXU_DTYPE), pltpu.VMEM((tm, d), F32), pltpu.VMEM((tm + SUBLANES, tc), F32)]
    if sample:
        in_specs += [pl.BlockSpec((None, tm, tc), lambda i, c: (layer, i, c))] * 2
        args += list(prev_rows)
        st_shape = jax.ShapeDtypeStruct((m, d_ff), F32)
        st_spec = pl.BlockSpec((tm, tc), lambda i, c: (i, c))
    else:
        st_shape = jax.ShapeDtypeStruct((m // tm, FFN_CONV - 1, d_ff), F32)
        st_spec = pl.BlockSpec((None, FFN_CONV - 1, tc), lambda i, c: (i, 0, c))
        scratch.append(pltpu.VMEM((n_chunks, SUBLANES, tc), F32))
    kern = functools.partial(_ffn_kernel, tm=tm, n_sub=max(1, tm // ROW_SUBTILE), n_chunks=n_chunks,
                             tiles_per_seq=tiles_per_seq,
                             dec_seq=dec_seq if sample else None, alpha=alpha)
    return pl.pallas_call(
        kern,
        out_shape=(jax.ShapeDtypeStruct((m, d), F32), st_shape),
        grid=(m // tm, n_chunks),
        in_specs=in_specs,
        out_specs=(pl.BlockSpec((tm, d), lambda i, c: (i, 0)), st_spec),
        scratch_shapes=scratch,
        compiler_params=_cparams("arbitrary", "arbitrary"),
        name="ffn_sample" if sample else "ffn_prompt",
    )(*args)


def _rope_tables(pos):
    half = ROPE_DIM // 2
    inv_freq = ROPE_THETA ** (-jnp.arange(half, dtype=F32) / half)
    ang = pos.astype(F32)[:, None] * inv_freq[None, :]
    cos, sin = jnp.cos(ang), jnp.sin(ang)
    n = pos.shape[0]
    ones = jnp.ones((n, HEAD_DIM - ROPE_DIM), F32)
    zeros = jnp.zeros((n, HEAD_DIM - ROPE_DIM), F32)
    zh = jnp.zeros((n, half), F32)
    return (jnp.concatenate([cos, cos, ones], axis=1),
            jnp.concatenate([-sin, zh, zeros], axis=1),
            jnp.concatenate([zh, sin, zeros], axis=1))


def _forward(x_prompt, x_sample, cache_k, cache_v, state_gdn, state_gdn_conv, state_ffn_conv, page_table,
             c_prompt, c_sample, w_ada_mix, b_ada_mix, w_in, gdn_conv_w, gdn_a_log, gdn_dt_bias, gdn_norm_g,
             w_out, ln_mix_g, ln_mix_b, w_ada_ffn, b_ada_ffn, w_up, ffn_conv_w, ffn_conv_b, w_down,
             ln_ffn_g, ln_ffn_b, *, past_len, tm_prompt, tm_in, tn_in, tc_ffn, tt_gdn):
    bsz, seq, d = x_prompt.shape
    dbsz, dseq, _ = x_sample.shape
    depth = w_in.shape[0]
    d_ff = w_down.shape[1]
    alpha = (2 * depth) ** 0.25
    n_pool = cache_k.shape[1]
    n_pages = page_table.shape[1]
    nb_past = past_len // MOBA_BLOCK
    assert d == 2 * WIDTH and past_len % MOBA_BLOCK == 0 and nb_past >= MOBA_TOPK
    assert n_pages * PAGE_SIZE == past_len and seq % MOBA_BLOCK == 0
    mp, ms = bsz * seq, dbsz * dseq
    tps = seq // tm_prompt

    gw, nh = WIDTH, N_HEADS

    w_in16 = w_in.astype(MXU_DTYPE)
    w_main = jnp.concatenate([w_in16[:, :, :4 * gw], w_in16[:, :, 4 * gw + 2 * nh:]], axis=2)
    w_ba = jnp.pad(w_in16[:, :, 4 * gw:4 * gw + 2 * nh], ((0, 0), (0, 0), (0, LANES - 2 * nh)))
    w_out16, w_up16, w_down16 = (w.astype(MXU_DTYPE) for w in (w_out, w_up, w_down))
    alog_pad = jnp.pad(gdn_a_log, ((0, 0), (nh, LANES - 2 * nh)))[:, None, :]
    dtb_pad = jnp.pad(gdn_dt_bias, ((0, 0), (nh, LANES - 2 * nh)))[:, None, :]

    c_all = jnp.concatenate([c_prompt, c_sample], axis=0)

    def mods(w_ada, b_ada):
        mod = _adaln(c_all, w_ada, b_ada).reshape(depth, bsz + dbsz, 3, d).transpose(0, 2, 1, 3)
        return mod[:, :, :bsz, None, :], jnp.repeat(mod[:, :, bsz:, :], dseq, axis=2)[:, :, None]

    mix_p, mix_s = mods(w_ada_mix, b_ada_mix)
    ffn_p, ffn_s = mods(w_ada_ffn, b_ada_ffn)
    lng, lnb, flng, flnb, ng, fcb = (a[:, None, :] for a in (ln_mix_g, ln_mix_b, ln_ffn_g, ln_ffn_b,
                                                               gdn_norm_g, ffn_conv_b))

    rope_p = _rope_tables(jnp.arange(seq))
    rope_s = _rope_tables(jnp.tile(past_len + jnp.arange(dseq), dbsz))
    pt_flat = page_table.reshape(-1)

    xp = x_prompt.reshape(mp, d)
    xs = x_sample.reshape(ms, d)
    p_idx = lambda i: i // tps
    zero_idx = lambda i: 0
    outs = [[] for _ in range(10)]
    tps_in = seq // tm_in
    fprev1 = jnp.broadcast_to(state_ffn_conv[:, :, 1:2, :], (depth, dbsz, dseq, d_ff)).reshape(depth, ms, d_ff)
    fprev2 = jnp.concatenate([state_ffn_conv, jnp.zeros((depth, dbsz, dseq - 2, d_ff), F32)],
                             axis=2).reshape(depth, ms, d_ff)
    for l in range(depth):
        g, ba, mq, mk, mv = _in_proj(xp, mix_p, w_main, w_ba, rope_p, l, tm_in, tn_in,
                                     lambda i: i // tps_in, lambda i: i % tps_in)
        o_gdn, s_p = _gdn_prompt(g, ba, gdn_conv_w, alog_pad, dtb_pad, ng, l, bsz, seq, tt_gdn)
        o_moba = _moba_prompt(mq, mk, mv, bsz, seq)
        xp = _out_ln(o_gdn, o_moba, xp, mix_p, w_out16, lng, lnb, l, tm_prompt, p_idx, alpha)
        xp, fcp = _ffn(xp, ffn_p, w_up16, w_down16, ffn_conv_w, fcb, flng, flnb, l, tm_prompt, tc_ffn, p_idx, alpha,
                       tiles_per_seq=tps)
        fcp = fcp[tps - 1::tps]
        gcp = g.reshape(bsz, seq, 4 * WIDTH)[:, seq - (GDN_CONV - 1):, :3 * WIDTH]

        gs, bas, mqs, mks, mvs = _in_proj(xs, mix_s, w_main, w_ba, rope_s, l, ms, WIDTH, zero_idx, zero_idx)
        gs3 = gs.reshape(dbsz, dseq, 4 * WIDTH)
        o_gdn_s, s_s = _gdn_sample(gs3, bas.reshape(dbsz, dseq, LANES), state_gdn_conv, state_gdn,
                                   gdn_conv_w, alog_pad, dtb_pad, ng, l, dbsz, dseq)
        mq3, mk3, mv3 = (a.reshape(dbsz, dseq, WIDTH) for a in (mqs, mks, mvs))
        mq8, mk8, mv8 = (jnp.pad(a, ((0, 0), (0, QPAD - dseq), (0, 0))) for a in (mq3, mk3, mv3))
        mq_hm = mq8.reshape(dbsz, QPAD, nh, HEAD_DIM).transpose(0, 2, 1, 3).reshape(dbsz, nh * QPAD, HEAD_DIM)
        kmean, pm, pls, pacc = _moba_partials(pt_flat, mq_hm, cache_k, cache_v, l, dbsz, nb_past, n_pages)
        o_moba_s = _moba_merge(mq8, mk8, mv8, kmean, pm, pls, pacc, dbsz, dseq, nb_past)
        xs = _out_ln(o_gdn_s.reshape(ms, WIDTH), o_moba_s.reshape(ms, WIDTH), xs, mix_s, w_out16, lng, lnb,
                     l, ms, zero_idx, alpha)
        xs, gfull = _ffn(xs, ffn_s, w_up16, w_down16, ffn_conv_w, fcb, flng, flnb, l, ms, tc_ffn, zero_idx, alpha,
                         prev_rows=(fprev1, fprev2), dec_seq=dseq)
        assert dseq >= GDN_CONV - 1
        gcs = gs3[:, dseq - (GDN_CONV - 1):, :3 * WIDTH]
        fcs = gfull.reshape(dbsz, dseq, d_ff)[:, dseq - (FFN_CONV - 1):, :]

        for lst, val in zip(outs, (mk.reshape(bsz, seq, nh, HEAD_DIM), mv.reshape(bsz, seq, nh, HEAD_DIM),
                                   mk3.reshape(dbsz, dseq, nh, HEAD_DIM), mv3.reshape(dbsz, dseq, nh, HEAD_DIM),
                                   s_p, s_s, gcp, gcs, fcp, fcs)):
            lst.append(val)

    return (xp.reshape(bsz, seq, d), xs.reshape(dbsz, dseq, d)) + tuple(jnp.stack(o) for o in outs)


def kernel(x_prompt, x_sample, cache_k, cache_v, state_gdn, state_gdn_conv, state_ffn_conv, page_table, c_prompt, c_sample, w_ada_mix, b_ada_mix, w_in, gdn_conv_w, gdn_a_log, gdn_dt_bias, gdn_norm_g, w_out, ln_mix_g, ln_mix_b, w_ada_ffn, b_ada_ffn, w_up, ffn_conv_w, ffn_conv_b, w_down, ln_ffn_g, ln_ffn_b):
    return _forward(x_prompt, x_sample, cache_k, cache_v, state_gdn, state_gdn_conv, state_ffn_conv, page_table,
                    c_prompt, c_sample, w_ada_mix, b_ada_mix, w_in, gdn_conv_w, gdn_a_log, gdn_dt_bias, gdn_norm_g,
                    w_out, ln_mix_g, ln_mix_b, w_ada_ffn, b_ada_ffn, w_up, ffn_conv_w, ffn_conv_b, w_down,
                    ln_ffn_g, ln_ffn_b, past_len=8192, tm_prompt=512, tm_in=1024, tn_in=512, tc_ffn=512, tt_gdn=256)
```
